```python
import math
import jax, jax.numpy as jnp
from jax import lax
import numpy as np


D_MODEL = 1024
BATCH = 16
SEQ = 2048
DEPTH = 2
DEC_BATCH = 128
DEC_SEQ = 1
PAST_LEN = 16384
PAGE_SIZE = 128

N_HEADS = 16
N_KV_HEADS = 2
HEAD_DIM = 64
GROUP = N_HEADS // N_KV_HEADS
WINDOW = 128
BLOCK = WINDOW
W_BUF = min(WINDOW, PAST_LEN)
ROPE_THETA = 10000.0
ATTN_SCALE = HEAD_DIM ** -0.5
GLA_HEADS = 4
GLA_DK = D_MODEL // 2
GLA_DV = D_MODEL
GLA_DKH = GLA_DK // GLA_HEADS
GLA_DVH = GLA_DV // GLA_HEADS
GLA_RANK = 16
GLA_TAU = 16.0
GLA_CHUNK = 64
D_FF = 2816
CONV_W = 3
N_ATTN = (DEPTH + 1) // 2
N_GLA = DEPTH // 2
ALPHA = (2 * DEPTH) ** 0.25
BETA = (8 * DEPTH) ** -0.25
LN_EPS = 1e-5
NORM_EPS = 1e-6
NEG_INF = -1e30

kernel_name = 'swa_sink_gla_convffn_deepnorm_step'


def layer_norm(x, g, b):
    xf = x.astype(jnp.float32)
    mu = jnp.mean(xf, -1, keepdims=True)
    var = jnp.mean(jnp.square(xf - mu), -1, keepdims=True)
    return ((xf - mu) * lax.rsqrt(var + LN_EPS) * g.astype(jnp.float32) + b.astype(jnp.float32)).astype(x.dtype)


def rope(x, pos):
    inv = 1.0 / (ROPE_THETA ** (jnp.arange(0, HEAD_DIM, 2, dtype=jnp.float32) / HEAD_DIM))
    ang = pos[:, None] * inv[None, :]
    cos = jnp.concatenate([jnp.cos(ang), jnp.cos(ang)], -1)[:, None, :]
    sin = jnp.concatenate([jnp.sin(ang), jnp.sin(ang)], -1)[:, None, :]
    xf = x.astype(jnp.float32)
    x1, x2 = jnp.split(xf, 2, axis=-1)
    rot = jnp.concatenate([-x2, x1], -1)
    return (xf * cos + rot * sin).astype(x.dtype)


def sink_softmax(scores, mask, sink):
    s = jnp.where(mask, scores, NEG_INF)
    m = jnp.maximum(jnp.max(s, -1, keepdims=True), sink)
    p = jnp.exp(s - m)
    return p / (jnp.sum(p, -1, keepdims=True) + jnp.exp(sink - m))


def qkv_proj(x, w_qkv, b_qkv, offset):
    B, T, _ = x.shape
    nq = N_HEADS * HEAD_DIM
    nk = N_KV_HEADS * HEAD_DIM
    qkv = x @ w_qkv + b_qkv
    q = qkv[..., :nq].reshape(B, T, N_HEADS, HEAD_DIM)
    k = qkv[..., nq:nq + nk].reshape(B, T, N_KV_HEADS, HEAD_DIM)
    v = qkv[..., nq + nk:].reshape(B, T, N_KV_HEADS, HEAD_DIM)
    pos = jnp.arange(T, dtype=jnp.float32) + offset
    return rope(q, pos), rope(k, pos), v


def swa_prompt(x, w_qkv, b_qkv, sinks, w_o):
    B, T, _ = x.shape
    q, k, v = qkv_proj(x, w_qkv, b_qkv, 0)
    nb = T // BLOCK
    qb = q.reshape(B, nb, BLOCK, N_KV_HEADS, GROUP, HEAD_DIM)
    kb = k.reshape(B, nb, BLOCK, N_KV_HEADS, HEAD_DIM)
    vb = v.reshape(B, nb, BLOCK, N_KV_HEADS, HEAD_DIM)

    def with_prev(t):
        prev = jnp.pad(t, ((0, 0), (1, 0), (0, 0), (0, 0), (0, 0)))[:, :nb]
        return jnp.concatenate([prev, t], axis=2)

    kk, vv = with_prev(kb), with_prev(vb)
    scores = jnp.einsum('bnqkgd,bnskd->bnkgqs', qb, kk).astype(jnp.float32) * ATTN_SCALE
    blk = jnp.arange(nb)[:, None, None]
    qpos = blk * BLOCK + jnp.arange(BLOCK)[None, :, None]
    kpos = (blk - 1) * BLOCK + jnp.arange(2 * BLOCK)[None, None, :]
    mask = (kpos >= 0) & (kpos <= qpos) & (qpos - kpos < WINDOW)
    sink = sinks.astype(jnp.float32).reshape(N_KV_HEADS, GROUP)[None, None, :, :, None, None]
    probs = sink_softmax(scores, mask[None, :, None, None], sink).astype(v.dtype)
    o = jnp.einsum('bnkgqs,bnskd->bnqkgd', probs, vv).reshape(B, T, N_HEADS * HEAD_DIM)
    return o @ w_o, k[:, T - W_BUF:], v[:, T - W_BUF:]


def swa_sample(x, k_buf, v_buf, w_qkv, b_qkv, sinks, w_o):
    B, T, _ = x.shape
    q, k, v = qkv_proj(x, w_qkv, b_qkv, PAST_LEN)
    kk = jnp.concatenate([k_buf.astype(k.dtype), k], axis=1)
    vv = jnp.concatenate([v_buf.astype(v.dtype), v], axis=1)
    qpos = PAST_LEN + jnp.arange(T)
    kpos = PAST_LEN - W_BUF + jnp.arange(W_BUF + T)
    mask = (kpos[None, :] <= qpos[:, None]) & (qpos[:, None] - kpos[None, :] < WINDOW)
    qg = q.reshape(B, T, N_KV_HEADS, GROUP, HEAD_DIM)
    scores = jnp.einsum('btkgd,bskd->bkgts', qg, kk).astype(jnp.float32) * ATTN_SCALE
    sink = sinks.astype(jnp.float32).reshape(N_KV_HEADS, GROUP)[None, :, :, None, None]
    probs = sink_softmax(scores, mask, sink).astype(v.dtype)
    o = jnp.einsum('bkgts,bskd->btkgd', probs, vv).reshape(B, T, N_HEADS * HEAD_DIM)
    return o @ w_o, kk[:, -W_BUF:], vv[:, -W_BUF:]


def gla_recurrence(q, k, v, log_a, s0, chunk):
    B, T, H, DK = q.shape
    DV = v.shape[-1]
    nc = T // chunk

    def to_chunks(t):
        return t.astype(jnp.float32).reshape(B, nc, chunk, H, t.shape[-1]).transpose(1, 0, 3, 2, 4)

    causal = jnp.tril(jnp.ones((chunk, chunk), dtype=bool))

    def step(S, inp):
        qc, kc, vc, lc = inp
        b = jnp.cumsum(lc, axis=2)
        b_last = b[:, :, -1:, :]
        q_in = qc * jnp.exp(b)
        k_in = kc * jnp.exp(-b)
        att = jnp.where(causal, jnp.einsum('bhtd,bhsd->bhts', q_in, k_in), 0.0)
        o = jnp.einsum('bhts,bhsv->bhtv', att, vc) + jnp.einsum('bhtd,bhdv->bhtv', q_in, S)
        S_new = jnp.exp(b_last[:, :, 0, :, None]) * S + jnp.einsum('bhsd,bhsv->bhdv', kc * jnp.exp(b_last - b), vc)
        return S_new, o

    S, o = lax.scan(step, s0.astype(jnp.float32), (to_chunks(q), to_chunks(k), to_chunks(v), to_chunks(log_a)))
    o = o.transpose(1, 0, 3, 2, 4).reshape(B, T, H, DV)
    return o, S


def gla_mix(x, s0, w_in, w_a1, w_a2, b_a, norm_g, w_o):
    B, T, _ = x.shape
    proj = x @ w_in
    q = proj[..., :GLA_DK].reshape(B, T, GLA_HEADS, GLA_DKH) * (GLA_DKH ** -0.5)
    k = proj[..., GLA_DK:2 * GLA_DK].reshape(B, T, GLA_HEADS, GLA_DKH)
    v = proj[..., 2 * GLA_DK:2 * GLA_DK + GLA_DV].reshape(B, T, GLA_HEADS, GLA_DVH)
    g = proj[..., 2 * GLA_DK + GLA_DV:].reshape(B, T, GLA_HEADS, GLA_DVH)
    log_a = jax.nn.log_sigmoid(((x @ w_a1) @ w_a2 + b_a).astype(jnp.float32)) / GLA_TAU
    log_a = log_a.reshape(B, T, GLA_HEADS, GLA_DKH)
    o, s_new = gla_recurrence(q, k, v, log_a, s0, min(GLA_CHUNK, T))
    o = o * lax.rsqrt(jnp.mean(jnp.square(o), -1, keepdims=True) + NORM_EPS) * norm_g.astype(jnp.float32)
    o = (o * jax.nn.silu(g.astype(jnp.float32))).astype(x.dtype).reshape(B, T, GLA_DV)
    return o @ w_o, s_new.astype(s0.dtype)


def conv_ffn(x, buf, w_up, conv_w, conv_b, w_down):
    T = x.shape[1]
    h = x @ w_up
    a, u = h[..., :D_FF], h[..., D_FF:]
    ap = jnp.concatenate([buf.astype(a.dtype), a], axis=1)
    c = conv_b + sum(conv_w[j] * ap[:, j:j + T] for j in range(CONV_W))
    y = (jax.nn.gelu(c, approximate=False) * u) @ w_down
    return y, ap[:, T:]


def setup_inputs(seed: int = 0) -> dict:
    key = jax.random.key(seed)
    ks = jax.random.split(key, 26)

    def nrm(k, shape, scale):
        return jax.random.normal(k, shape, jnp.float32) * scale

    qkv_w = (N_HEADS + 2 * N_KV_HEADS) * HEAD_DIM
    gla_in = 2 * GLA_DK + 2 * GLA_DV
    return {
        'x_prompt': nrm(ks[0], (BATCH, SEQ, D_MODEL), 1.0),
        'x_sample': nrm(ks[1], (DEC_BATCH, DEC_SEQ, D_MODEL), 1.0),
        'cache_k': nrm(ks[2], (N_ATTN, DEC_BATCH, W_BUF, N_KV_HEADS, HEAD_DIM), 1.0),
        'cache_v': nrm(ks[3], (N_ATTN, DEC_BATCH, W_BUF, N_KV_HEADS, HEAD_DIM), 1.0),
        'state_gla': nrm(ks[4], (N_GLA, DEC_BATCH, GLA_HEADS, GLA_DKH, GLA_DVH), 1.0),
        'state_conv': nrm(ks[5], (DEPTH, DEC_BATCH, CONV_W - 1, D_FF), 1.0),
        'attn_w_qkv': nrm(ks[6], (N_ATTN, D_MODEL, qkv_w), D_MODEL ** -0.5),
        'attn_b_qkv': nrm(ks[7], (N_ATTN, qkv_w), 0.02),
        'attn_sinks': nrm(ks[8], (N_ATTN, N_HEADS), 1.0),
        'attn_w_o': nrm(ks[9], (N_ATTN, N_HEADS * HEAD_DIM, D_MODEL), BETA * (N_HEADS * HEAD_DIM) ** -0.5),
        'gla_w_in': nrm(ks[10], (N_GLA, D_MODEL, gla_in), D_MODEL ** -0.5),
        'gla_w_a1': nrm(ks[11], (N_GLA, D_MODEL, GLA_RANK), D_MODEL ** -0.5),
        'gla_w_a2': nrm(ks[12], (N_GLA, GLA_RANK, GLA_DK), GLA_RANK ** -0.5),
        'gla_b_a': nrm(ks[13], (N_GLA, GLA_DK), 0.1),
        'gla_norm_g': 1.0 + nrm(ks[14], (N_GLA, GLA_DVH), 0.02),
        'gla_w_o': nrm(ks[15], (N_GLA, GLA_DV, D_MODEL), BETA * GLA_DV ** -0.5),
        'ffn_w_up': nrm(ks[16], (DEPTH, D_MODEL, 2 * D_FF), D_MODEL ** -0.5),
        'ffn_conv_w': nrm(ks[17], (DEPTH, CONV_W, D_FF), CONV_W ** -0.5),
        'ffn_conv_b': nrm(ks[18], (DEPTH, D_FF), 0.02),
        'ffn_w_down': nrm(ks[19], (DEPTH, D_FF, D_MODEL), BETA * D_FF ** -0.5),
        'ln_mix_g': 1.0 + nrm(ks[20], (DEPTH, D_MODEL), 0.02),
        'ln_mix_b': nrm(ks[21], (DEPTH, D_MODEL), 0.02),
        'ln_ffn_g': 1.0 + nrm(ks[22], (DEPTH, D_MODEL), 0.02),
        'ln_ffn_b': nrm(ks[23], (DEPTH, D_MODEL), 0.02),
    }


def reference(x_prompt, x_sample, cache_k, cache_v, state_gla, state_conv,
              attn_w_qkv, attn_b_qkv, attn_sinks, attn_w_o,
              gla_w_in, gla_w_a1, gla_w_a2, gla_b_a, gla_norm_g, gla_w_o,
              ffn_w_up, ffn_conv_w, ffn_conv_b, ffn_w_down,
              ln_mix_g, ln_mix_b, ln_ffn_g, ln_ffn_b):
    xp, xs = x_prompt, x_sample
    kp_l, vp_l, ks_l, vs_l = [], [], [], []
    sp_l, ss_l, cp_l, cs_l = [], [], [], []
    for i in range(DEPTH):
        j = i // 2
        if i % 2 == 0:
            mp, kp, vp = swa_prompt(xp, attn_w_qkv[j], attn_b_qkv[j], attn_sinks[j], attn_w_o[j])
            ms, k_s, v_s = swa_sample(xs, cache_k[j], cache_v[j], attn_w_qkv[j], attn_b_qkv[j], attn_sinks[j], attn_w_o[j])
            kp_l.append(kp)
            vp_l.append(vp)
            ks_l.append(k_s)
            vs_l.append(v_s)
        else:
            s0 = jnp.zeros((xp.shape[0], GLA_HEADS, GLA_DKH, GLA_DVH), state_gla.dtype)
            mp, sp = gla_mix(xp, s0, gla_w_in[j], gla_w_a1[j], gla_w_a2[j], gla_b_a[j], gla_norm_g[j], gla_w_o[j])
            ms, ss = gla_mix(xs, state_gla[j], gla_w_in[j], gla_w_a1[j], gla_w_a2[j], gla_b_a[j], gla_norm_g[j], gla_w_o[j])
            sp_l.append(sp)
            ss_l.append(ss)
        xp = layer_norm(ALPHA * xp + mp, ln_mix_g[i], ln_mix_b[i])
        xs = layer_norm(ALPHA * xs + ms, ln_mix_g[i], ln_mix_b[i])
        buf0 = jnp.zeros((xp.shape[0], CONV_W - 1, D_FF), state_conv.dtype)
        fp, cp = conv_ffn(xp, buf0, ffn_w_up[i], ffn_conv_w[i], ffn_conv_b[i], ffn_w_down[i])
        fs, cs = conv_ffn(xs, state_conv[i], ffn_w_up[i], ffn_conv_w[i], ffn_conv_b[i], ffn_w_down[i])
        cp_l.append(cp)
        cs_l.append(cs)
        xp = layer_norm(ALPHA * xp + fp, ln_ffn_g[i], ln_ffn_b[i])
        xs = layer_norm(ALPHA * xs + fs, ln_ffn_g[i], ln_ffn_b[i])
    return (xp, xs, jnp.stack(kp_l), jnp.stack(vp_l), jnp.stack(ks_l), jnp.stack(vs_l),
            jnp.stack(sp_l), jnp.stack(ss_l), jnp.stack(cp_l), jnp.stack(cs_l))
```

```python
import functools

import jax
import jax.numpy as jnp
from jax import lax
from jax.experimental import pallas as pl
from jax.experimental.pallas import tpu as pltpu

F32 = jnp.float32
BF16 = jnp.bfloat16

D_MODEL = 1024
DEPTH = 2
PAST_LEN = 16384
N_HEADS = 16
N_KV_HEADS = 2
HEAD_DIM = 64
GROUP = N_HEADS // N_KV_HEADS
WINDOW = 128
ROPE_THETA = 10000.0
ATTN_SCALE = HEAD_DIM ** -0.5
GLA_HEADS = 4
GLA_DK = D_MODEL // 2
GLA_DV = D_MODEL
GLA_DKH = GLA_DK // GLA_HEADS
GLA_DVH = GLA_DV // GLA_HEADS
GLA_RANK = 16
GLA_TAU = 16.0
GLA_CHUNK = 64
D_FF = 2816
CONV_W = 3
ALPHA = (2 * DEPTH) ** 0.25
LN_EPS = 1e-5
NORM_EPS = 1e-6
NEG_INF = -1e30

LANES = 128
SUBLANES = 8
VMEM_LIMIT_BYTES = 56 * 1024 * 1024

FFN_TM = 512
FFN_TF = 256
GLA_TM = 512
SAMPLE_SEQ_BLOCK = 8


def _cparams(sem):
    return pltpu.CompilerParams(dimension_semantics=sem, vmem_limit_bytes=VMEM_LIMIT_BYTES)


def _resident(shape):
    nd = len(shape)
    return pl.BlockSpec(shape, lambda *_: (0,) * nd, pipeline_mode=pl.Buffered(1))


def _layer_norm(z, g, b):
    mu = jnp.mean(z, axis=-1, keepdims=True)
    zc = z - mu
    var = jnp.mean(zc * zc, axis=-1, keepdims=True)
    return zc * lax.rsqrt(var + LN_EPS) * g + b


def _gelu(c):
    return 0.5 * c * (1.0 + lax.erf(c * (2.0 ** -0.5)))


def _dot(a, b):
    return jnp.dot(a, b, preferred_element_type=F32)


def _dot_nt(a, b):
    return lax.dot_general(a, b, (((1,), (1,)), ((), ())), preferred_element_type=F32)


def _dot_tn(a, b):
    return lax.dot_general(a, b, (((0,), (0,)), ((), ())), preferred_element_type=F32)


def _rope_tables(pos):
    inv = 1.0 / (ROPE_THETA ** (jnp.arange(0, HEAD_DIM, 2, dtype=F32) / HEAD_DIM))
    ang = pos[:, None] * inv[None, :]
    cos, sin = jnp.cos(ang), jnp.sin(ang)
    zero = jnp.zeros_like(sin)
    cos_h = jnp.concatenate([cos, cos], -1)
    sa_h = jnp.concatenate([-sin, zero], -1)
    sb_h = jnp.concatenate([zero, sin], -1)
    two = lambda t: jnp.concatenate([t, t], -1)
    return two(cos_h), two(sa_h), two(sb_h)


def _rope_slab(s, cos, sa, sb):
    return (s * cos + pltpu.roll(s, LANES - HEAD_DIM // 2, 1) * sa
            + pltpu.roll(s, HEAD_DIM // 2, 1) * sb)


def _sink_softmax(s, sink):
    m = jnp.maximum(jnp.max(s, axis=-1, keepdims=True), sink)
    p = jnp.exp(s - m)
    den = jnp.sum(p, axis=-1, keepdims=True) + jnp.exp(sink - m)
    return p * (1.0 / den)


def _lane_split_pair(t):
    lo = lax.broadcasted_iota(jnp.int32, t.shape, 1) < HEAD_DIM
    tr = pltpu.roll(t, HEAD_DIM, 1)
    z = jnp.zeros_like(t)
    return (jnp.where(lo, t, z), jnp.where(lo, z, tr), jnp.where(lo, tr, z), jnp.where(lo, z, t))


def _attn_prompt_kernel(sinks_ref, x_ref, cos_ref, sa_ref, sb_ref, wqkv_ref, bqkv_ref, wo_ref,
                        g_ref, b_ref, y_ref, kn_ref, vn_ref, kprev_ref, vprev_ref):
    blk = pl.program_id(1)
    nblk = pl.num_programs(1)
    nq = N_HEADS * HEAD_DIM

    @pl.when(blk == 0)
    def _():
        kprev_ref[...] = jnp.zeros_like(kprev_ref)
        vprev_ref[...] = jnp.zeros_like(vprev_ref)

    x = x_ref[0]
    qkv = _dot(x.astype(BF16), wqkv_ref[...]) + bqkv_ref[...]
    cos, sa, sb = cos_ref[...], sa_ref[...], sb_ref[...]
    k_cur = _rope_slab(qkv[:, nq:nq + LANES], cos, sa, sb)
    v_cur = qkv[:, nq + LANES:nq + 2 * LANES]
    kk = jnp.concatenate([kprev_ref[...], k_cur], axis=0).astype(BF16)
    vv = jnp.concatenate([vprev_ref[...], v_cur], axis=0).astype(BF16)
    kmats = _lane_split_pair(kk)
    vmats = _lane_split_pair(vv)

    t = lax.broadcasted_iota(jnp.int32, (WINDOW, 2 * WINDOW), 0)
    s = lax.broadcasted_iota(jnp.int32, (WINDOW, 2 * WINDOW), 1)
    mask = (s > t) & (s <= t + WINDOW) & ((blk > 0) | (s >= WINDOW))

    nslab = nq // LANES
    slabs_per_group = nslab // N_KV_HEADS
    o_slabs = []
    for g in range(N_KV_HEADS):
        qs = []
        for j in range(g * slabs_per_group, (g + 1) * slabs_per_group):
            qj = _rope_slab(qkv[:, j * LANES:(j + 1) * LANES], cos, sa, sb) * ATTN_SCALE
            qs.append(qj.astype(BF16))
        qg = jnp.concatenate(qs, axis=0)
        og = None
        for par in range(2):
            sc = _dot_nt(qg, kmats[2 * g + par])
            ps = []
            for i in range(slabs_per_group):
                head = 2 * (g * slabs_per_group + i) + par
                si = jnp.where(mask, sc[i * WINDOW:(i + 1) * WINDOW], NEG_INF)
                ps.append(_sink_softmax(si, sinks_ref[head]).astype(BF16))
            pv = _dot(jnp.concatenate(ps, axis=0), vmats[2 * g + par])
            og = pv if og is None else og + pv
        for i in range(slabs_per_group):
            o_slabs.append(og[i * WINDOW:(i + 1) * WINDOW])
    o = jnp.concatenate(o_slabs, axis=1).astype(BF16)
    z = ALPHA * x + _dot(o, wo_ref[...])
    y_ref[0] = _layer_norm(z, g_ref[...], b_ref[...])

    kprev_ref[...] = k_cur
    vprev_ref[...] = v_cur

    @pl.when(blk == nblk - 1)
    def _():
        kn_ref[0] = k_cur
        vn_ref[0] = v_cur


def _attn_prompt(x, wqkv, bqkv, sinks, wo, g, b):
    bsz, seq, d = x.shape
    nblk = seq // WINDOW
    nqkv = wqkv.shape[1]
    cos, sa, sb = _rope_tables(jnp.arange(seq, dtype=F32))
    tab = pl.BlockSpec((WINDOW, LANES), lambda bi, qi: (qi, 0))
    kv_out = pl.BlockSpec((1, WINDOW, LANES), lambda bi, qi: (bi, 0, 0))
    return pl.pallas_call(
        _attn_prompt_kernel,
        grid=(bsz, nblk),
        in_specs=[
            pl.BlockSpec(memory_space=pltpu.SMEM),
            pl.BlockSpec((1, WINDOW, d), lambda bi, qi: (bi, qi, 0)),
            tab, tab, tab,
            _resident((d, nqkv)), _resident((1, nqkv)), _resident((d, d)),
            _resident((1, d)), _resident((1, d)),
        ],
        out_specs=[pl.BlockSpec((1, WINDOW, d), lambda bi, qi: (bi, qi, 0)), kv_out, kv_out],
        out_shape=[jax.ShapeDtypeStruct((bsz, seq, d), F32),
                   jax.ShapeDtypeStruct((bsz, WINDOW, LANES), F32),
                   jax.ShapeDtypeStruct((bsz, WINDOW, LANES), F32)],
        scratch_shapes=[pltpu.VMEM((WINDOW, LANES), F32), pltpu.VMEM((WINDOW, LANES), F32)],
        compiler_params=_cparams(("arbitrary", "arbitrary")),
        name="attn_prompt",
    )(sinks, x, cos, sa, sb, wqkv, bqkv, wo, g, b)


def _conv_gate(a, a1, a2, u, cw, cb):
    c = cb + cw[0:1] * a2 + cw[1:2] * a1 + cw[2:3] * a
    return (_gelu(c) * u).astype(BF16)


def _ffn_prompt_kernel(x_ref, wa_ref, wu_ref, wd_ref, cw_ref, cb_ref, g_ref, b_ref,
                       y_ref, cs_ref, carry_ref, acc_ref, *, tiles_per_seq):
    i = pl.program_id(0)
    pos = i % tiles_per_seq

    @pl.when(pos == 0)
    def _():
        carry_ref[...] = jnp.zeros_like(carry_ref)

    x = x_ref[...]
    xb = x.astype(BF16)
    tm = x.shape[0]
    nchunk, _, tf = wa_ref.shape
    row = lax.broadcasted_iota(jnp.int32, (tm, tf), 0)
    for j in range(nchunk):
        sl = slice(j * tf, (j + 1) * tf)
        a = _dot(xb, wa_ref[j])
        u = _dot(xb, wu_ref[j])
        prev = carry_ref[:, sl]
        p1 = prev[SUBLANES - 1:SUBLANES]
        p2 = prev[SUBLANES - 2:SUBLANES - 1]
        a1 = jnp.where(row == 0, p1, pltpu.roll(a, 1, 0))
        a2 = jnp.where(row == 0, p2, jnp.where(row == 1, p1, pltpu.roll(a, 2, 0)))
        carry_ref[:, sl] = a[tm - SUBLANES:tm]
        h = _conv_gate(a, a1, a2, u, cw_ref[:, sl], cb_ref[:, sl])
        dn = _dot(h, wd_ref[j])
        if j == 0:
            acc_ref[...] = dn
        else:
            acc_ref[...] += dn
    z = ALPHA * x + acc_ref[...]
    y_ref[...] = _layer_norm(z, g_ref[...], b_ref[...])

    @pl.when(pos == tiles_per_seq - 1)
    def _():
        cs_ref[0] = carry_ref[SUBLANES - (CONV_W - 1):SUBLANES, :]


def _ffn_prompt(x, seq, wa, wu, wd, cw, cb, g, b):
    n, d = x.shape
    tm = min(FFN_TM, seq)
    tiles_per_seq = seq // tm
    nchunk, _, tf = wa.shape
    dff = nchunk * tf
    return pl.pallas_call(
        functools.partial(_ffn_prompt_kernel, tiles_per_seq=tiles_per_seq),
        grid=(n // tm,),
        in_specs=[
            pl.BlockSpec((tm, d), lambda i: (i, 0)),
            _resident(wa.shape), _resident(wu.shape), _resident(wd.shape),
            _resident(cw.shape), _resident(cb.shape), _resident(g.shape), _resident(b.shape),
        ],
        out_specs=[pl.BlockSpec((tm, d), lambda i: (i, 0)),
                   pl.BlockSpec((1, CONV_W - 1, dff), lambda i: (i // tiles_per_seq, 0, 0))],
        out_shape=[jax.ShapeDtypeStruct((n, d), F32),
                   jax.ShapeDtypeStruct((n // seq, CONV_W - 1, dff), F32)],
        scratch_shapes=[pltpu.VMEM((SUBLANES, dff), F32), pltpu.VMEM((tm, d), F32)],
        compiler_params=_cparams(("arbitrary",)),
        name="ffn_prompt",
    )(x, wa, wu, wd, cw, cb, g, b)


def _ffn_sample_kernel(x_ref, s0_ref, s1_ref, wa_ref, wu_ref, wd_ref, cw_ref, cb_ref, g_ref, b_ref,
                       y_ref, a_ref, acc_ref):
    j = pl.program_id(0)
    x = x_ref[...]
    xb = x.astype(BF16)
    a = _dot(xb, wa_ref[0])
    u = _dot(xb, wu_ref[0])
    a_ref[...] = a
    h = _conv_gate(a, s1_ref[...], s0_ref[...], u, cw_ref[...], cb_ref[...])
    dn = _dot(h, wd_ref[0])

    @pl.when(j == 0)
    def _():
        acc_ref[...] = dn

    @pl.when(j > 0)
    def _():
        acc_ref[...] += dn

    @pl.when(j == pl.num_programs(0) - 1)
    def _():
        y_ref[...] = _layer_norm(ALPHA * x + acc_ref[...], g_ref[...], b_ref[...])


def _ffn_sample(x, s0, s1, wa, wu, wd, cw, cb, g, b):
    n, d = x.shape
    nchunk, _, tf = wa.shape
    dff = nchunk * tf
    col = pl.BlockSpec((n, tf), lambda j: (0, j))
    return pl.pallas_call(
        _ffn_sample_kernel,
        grid=(nchunk,),
        in_specs=[
            pl.BlockSpec((n, d), lambda j: (0, 0)), col, col,
            pl.BlockSpec((1, d, tf), lambda j: (j, 0, 0)),
            pl.BlockSpec((1, d, tf), lambda j: (j, 0, 0)),
            pl.BlockSpec((1, tf, d), lambda j: (j, 0, 0)),
            pl.BlockSpec((CONV_W, tf), lambda j: (0, j)),
            pl.BlockSpec((1, tf), lambda j: (0, j)),
            pl.BlockSpec((1, d), lambda j: (0, 0)), pl.BlockSpec((1, d), lambda j: (0, 0)),
        ],
        out_specs=[pl.BlockSpec((n, d), lambda j: (0, 0)), col],
        out_shape=[jax.ShapeDtypeStruct((n, d), F32), jax.ShapeDtypeStruct((n, dff), F32)],
        scratch_shapes=[pltpu.VMEM((n, d), F32)],
        compiler_params=_cparams(("arbitrary",)),
        name="ffn_sample",
    )(x, s0, s1, wa, wu, wd, cw, cb, g, b)


def _gla_decay_logits(xb, wa1_ref, wa2_ref, ba_ref):
    t1 = _dot(xb, wa1_ref[...])
    z = _dot(t1.astype(BF16), wa2_ref[...]) + ba_ref[...]
    return jax.nn.log_sigmoid(z) * (1.0 / GLA_TAU)


def _gla_out_gate(o, gate, ng):
    outs = []
    for h in range(GLA_HEADS):
        oh = o[:, h * GLA_DVH:(h + 1) * GLA_DVH]
        ms = jnp.mean(oh * oh, axis=-1, keepdims=True)
        outs.append(oh * lax.rsqrt(ms + NORM_EPS) * ng)
    on = jnp.concatenate(outs, axis=1)
    return (on * jax.nn.silu(gate)).astype(BF16)


def _gla_prompt_kernel(x_ref, win_ref, wa1_ref, wa2_ref, ba_ref, ng_ref, wo_ref, g_ref, b_ref,
                       y_ref, so_ref, s_ref, proj_ref, la_ref, o_ref):
    ti = pl.program_id(1)
    ck = GLA_CHUNK

    @pl.when(ti == 0)
    def _():
        s_ref[...] = jnp.zeros_like(s_ref)

    x = x_ref[0]
    xb = x.astype(BF16)
    tm = x.shape[0]
    proj_ref[...] = _dot(xb, win_ref[...])
    la_ref[...] = _gla_decay_logits(xb, wa1_ref, wa2_ref, ba_ref)

    r = lax.broadcasted_iota(jnp.int32, (ck, ck), 0)
    c = lax.broadcasted_iota(jnp.int32, (ck, ck), 1)
    causal = r >= c
    tril = causal.astype(F32)

    def chunk(ci, carry):
        r0 = pl.multiple_of(ci * ck, ck)
        rows = pl.ds(r0, ck)
        bc = jnp.dot(tril, la_ref[rows, :], preferred_element_type=F32,
                     precision=lax.Precision.HIGHEST)
        for h in range(GLA_HEADS):
            bh = bc[:, h * GLA_DKH:(h + 1) * GLA_DKH]
            q = proj_ref[rows, h * GLA_DKH:(h + 1) * GLA_DKH] * (GLA_DKH ** -0.5)
            k = proj_ref[rows, GLA_DK + h * GLA_DKH:GLA_DK + (h + 1) * GLA_DKH]
            v = proj_ref[rows, 2 * GLA_DK + h * GLA_DVH:2 * GLA_DK + (h + 1) * GLA_DVH].astype(BF16)
            q_in = (q * jnp.exp(bh)).astype(BF16)
            k_in = (k * jnp.exp(-bh)).astype(BF16)
            att = jnp.where(causal, _dot_nt(q_in, k_in), 0.0)
            s_old = s_ref[h]
            o_ref[rows, h * GLA_DVH:(h + 1) * GLA_DVH] = (
                _dot(att.astype(BF16), v) + _dot(q_in, s_old.astype(BF16)))
            bl = bh[ck - 1:ck]
            kd = (k * jnp.exp(bl - bh)).astype(BF16)
            dcol = jnp.transpose(jnp.broadcast_to(jnp.exp(bl), (GLA_DKH, GLA_DKH)))
            dfull = jnp.concatenate([dcol] * (GLA_DVH // GLA_DKH), axis=1)
            s_ref[h] = dfull * s_old + _dot_tn(kd, v)
        return carry

    lax.fori_loop(0, tm // ck, chunk, 0)

    og = _gla_out_gate(o_ref[...], proj_ref[:, 2 * GLA_DK + GLA_DV:], ng_ref[...])
    z = ALPHA * x + _dot(og, wo_ref[...])
    y_ref[0] = _layer_norm(z, g_ref[...], b_ref[...])

    @pl.when(ti == pl.num_programs(1) - 1)
    def _():
        so_ref[0] = s_ref[...]


def _gla_prompt(x, win, wa1, wa2, ba, ng, wo, g, b):
    bsz, seq, d = x.shape
    tm = min(GLA_TM, seq)
    nin = win.shape[1]
    st = (GLA_HEADS, GLA_DKH, GLA_DVH)
    return pl.pallas_call(
        _gla_prompt_kernel,
        grid=(bsz, seq // tm),
        in_specs=[
            pl.BlockSpec((1, tm, d), lambda bi, ti: (bi, ti, 0)),
            _resident(win.shape), _resident(wa1.shape), _resident(wa2.shape), _resident(ba.shape),
            _resident(ng.shape), _resident(wo.shape), _resident(g.shape), _resident(b.shape),
        ],
        out_specs=[pl.BlockSpec((1, tm, d), lambda bi, ti: (bi, ti, 0)),
                   pl.BlockSpec((1,) + st, lambda bi, ti: (bi, 0, 0, 0))],
        out_shape=[jax.ShapeDtypeStruct((bsz, seq, d), F32),
                   jax.ShapeDtypeStruct((bsz,) + st, F32)],
        scratch_shapes=[pltpu.VMEM(st, F32), pltpu.VMEM((tm, nin), F32),
                        pltpu.VMEM((tm, GLA_DK), F32), pltpu.VMEM((tm, GLA_DV), F32)],
        compiler_params=_cparams(("arbitrary", "arbitrary")),
        name="gla_prompt",
    )(x, win, wa1, wa2, ba, ng, wo, g, b)


def _linear_kernel(x_ref, w_ref, b_ref, o_ref):
    o_ref[...] = _dot(x_ref[...].astype(BF16), w_ref[...]) + b_ref[...]


def _linear(x, w, bias):
    n, kdim = x.shape
    nout = w.shape[1]
    tn = next(t for t in (512, 256, 128) if nout % t == 0)
    return pl.pallas_call(
        _linear_kernel,
        grid=(nout // tn,),
        in_specs=[pl.BlockSpec((n, kdim), lambda j: (0, 0)),
                  pl.BlockSpec((kdim, tn), lambda j: (0, j)),
                  pl.BlockSpec((1, tn), lambda j: (0, j))],
        out_specs=pl.BlockSpec((n, tn), lambda j: (0, j)),
        out_shape=jax.ShapeDtypeStruct((n, nout), F32),
        compiler_params=_cparams(("arbitrary",)),
        name="sample_linear",
    )(x, w, bias)


def _linear_ln_kernel(h_ref, w_ref, r_ref, g_ref, b_ref, o_ref):
    z = ALPHA * r_ref[...] + _dot(h_ref[...].astype(BF16), w_ref[...])
    o_ref[...] = _layer_norm(z, g_ref[...], b_ref[...])


def _linear_ln(h, w, resid, g, b):
    n, d = resid.shape
    full = lambda a: pl.BlockSpec(a.shape, lambda i: (0,) * a.ndim)
    return pl.pallas_call(
        _linear_ln_kernel,
        grid=(1,),
        in_specs=[full(h), full(w), full(resid), full(g), full(b)],
        out_specs=pl.BlockSpec((n, d), lambda i: (0, 0)),
        out_shape=jax.ShapeDtypeStruct((n, d), F32),
        compiler_params=_cparams(("arbitrary",)),
        name="sample_linear_ln",
    )(h, w, resid, g, b)


def _attn_sample_kernel(qkv_ref, ck_ref, cv_ref, cos_ref, sa_ref, sb_ref, sink_ref,
                        o_ref, kn_ref, vn_ref):
    nq = N_HEADS * HEAD_DIM
    wbuf = ck_ref.shape[1]
    cos, sa, sb = cos_ref[...], sa_ref[...], sb_ref[...]
    qkv = qkv_ref[...]
    q_all = jnp.concatenate(
        [_rope_slab(qkv[:, j * LANES:(j + 1) * LANES], cos, sa, sb) for j in range(nq // LANES)],
        axis=1) * ATTN_SCALE
    k_all = _rope_slab(qkv[:, nq:nq + LANES], cos, sa, sb)
    v_all = qkv[:, nq + LANES:nq + 2 * LANES]

    head_of_lane = lax.broadcasted_iota(jnp.int32, (N_HEADS, nq), 1) // HEAD_DIM
    head_of_row = lax.broadcasted_iota(jnp.int32, (N_HEADS, nq), 0)
    own = head_of_lane == head_of_row
    lo = lax.broadcasted_iota(jnp.int32, (wbuf, LANES), 1) < HEAD_DIM
    last = lax.broadcasted_iota(jnp.int32, (wbuf, LANES), 0) == wbuf - 1
    reps = N_HEADS // N_KV_HEADS * HEAD_DIM // LANES

    def widen(t):
        tr = pltpu.roll(t, HEAD_DIM, 1)
        t0 = jnp.where(lo, t, tr)
        t1 = jnp.where(lo, tr, t)
        return jnp.concatenate([t0] * reps + [t1] * reps, axis=1).astype(BF16)

    sink = sink_ref[...]
    rows = []
    for s in range(qkv.shape[0]):
        k_new = jnp.where(last, k_all[s:s + 1], pltpu.roll(ck_ref[s], wbuf - 1, 0))
        v_new = jnp.where(last, v_all[s:s + 1], pltpu.roll(cv_ref[s], wbuf - 1, 0))
        kn_ref[s] = k_new
        vn_ref[s] = v_new
        qm = jnp.where(own, q_all[s:s + 1], 0.0).astype(BF16)
        sc = _dot_nt(qm, widen(k_new))
        p = _sink_softmax(sc, sink).astype(BF16)
        o16 = _dot(p, widen(v_new))
        rows.append(jnp.sum(jnp.where(own, o16, 0.0), axis=0, keepdims=True))
    o_ref[...] = jnp.concatenate(rows, axis=0)


def _attn_sample(qkv, ck, cv, sinks):
    n = qkv.shape[0]
    wbuf = ck.shape[1]
    sb_ = SAMPLE_SEQ_BLOCK
    nq = N_HEADS * HEAD_DIM
    cos, sa, sb = _rope_tables(jnp.full((1,), PAST_LEN, dtype=F32))
    one = lambda a: pl.BlockSpec(a.shape, lambda i: (0,) * a.ndim)
    sink_col = sinks.reshape(N_HEADS, 1)
    cache = pl.BlockSpec((sb_, wbuf, LANES), lambda i: (i, 0, 0))
    return pl.pallas_call(
        _attn_sample_kernel,
        grid=(n // sb_,),
        in_specs=[pl.BlockSpec((sb_, qkv.shape[1]), lambda i: (i, 0)), cache, cache,
                  one(cos), one(sa), one(sb), one(sink_col)],
        out_specs=[pl.BlockSpec((sb_, nq), lambda i: (i, 0)), cache, cache],
        out_shape=[jax.ShapeDtypeStruct((n, nq), F32),
                   jax.ShapeDtypeStruct(ck.shape, F32), jax.ShapeDtypeStruct(cv.shape, F32)],
        compiler_params=_cparams(("arbitrary",)),
        name="attn_sample",
    )(qkv, ck, cv, cos, sa, sb, sink_col)


def _gla_sample_kernel(proj_ref, z_ref, s0_ref, ng_ref, o_ref, sn_ref, oacc_ref):
    proj = proj_ref[...]
    la = jax.nn.log_sigmoid(z_ref[...]) * (1.0 / GLA_TAU)
    nseq = proj.shape[0]
    pad = jnp.zeros((LANES - nseq, GLA_DKH), F32)

    def cols(t):
        return jnp.transpose(jnp.concatenate([t, pad], axis=0))

    for h in range(GLA_HEADS):
        b = la[:, h * GLA_DKH:(h + 1) * GLA_DKH]
        q = proj[:, h * GLA_DKH:(h + 1) * GLA_DKH] * (GLA_DKH ** -0.5)
        k = proj[:, GLA_DK + h * GLA_DKH:GLA_DK + (h + 1) * GLA_DKH]
        v = proj[:, 2 * GLA_DK + h * GLA_DVH:2 * GLA_DK + (h + 1) * GLA_DVH]
        eb = jnp.exp(b)
        q_in = q * eb
        k_in = k * jnp.exp(-b)
        att = jnp.sum(q_in * k_in, axis=-1, keepdims=True)
        kd = k * jnp.exp(b - b)
        qc, kc, ec = cols(q_in), cols(kd), cols(eb)
        for s in range(nseq):
            s_old = s0_ref[s, h]
            vrow = v[s:s + 1]
            o_row = att[s:s + 1] * vrow + jnp.sum(qc[:, s:s + 1] * s_old, axis=0, keepdims=True)
            oacc_ref[s:s + 1, h * GLA_DVH:(h + 1) * GLA_DVH] = o_row
            sn_ref[s, h] = ec[:, s:s + 1] * s_old + kc[:, s:s + 1] * vrow
    og = _gla_out_gate(oacc_ref[...], proj[:, 2 * GLA_DK + GLA_DV:], ng_ref[...])
    o_ref[...] = og.astype(F32)


def _gla_sample(proj, z, s0, ng):
    n = proj.shape[0]
    sb_ = SAMPLE_SEQ_BLOCK
    st = pl.BlockSpec((sb_,) + s0.shape[1:], lambda i: (i, 0, 0, 0))
    return pl.pallas_call(
        _gla_sample_kernel,
        grid=(n // sb_,),
        in_specs=[pl.BlockSpec((sb_, proj.shape[1]), lambda i: (i, 0)),
                  pl.BlockSpec((sb_, z.shape[1]), lambda i: (i, 0)), st,
                  pl.BlockSpec(ng.shape, lambda i: (0, 0))],
        out_specs=[pl.BlockSpec((sb_, GLA_DV), lambda i: (i, 0)), st],
        out_shape=[jax.ShapeDtypeStruct((n, GLA_DV), F32), jax.ShapeDtypeStruct(s0.shape, F32)],
        scratch_shapes=[pltpu.VMEM((sb_, GLA_DV), F32)],
        compiler_params=_cparams(("arbitrary",)),
        name="gla_sample",
    )(proj, z, s0, ng)


def _row(v):
    return v.reshape(1, -1)


def _ffn_weights(w_up, w_down):
    d = w_up.shape[0]
    nchunk = D_FF // FFN_TF
    wa = w_up[:, :D_FF].reshape(d, nchunk, FFN_TF).transpose(1, 0, 2).astype(BF16)
    wu = w_up[:, D_FF:].reshape(d, nchunk, FFN_TF).transpose(1, 0, 2).astype(BF16)
    wd = w_down.reshape(nchunk, FFN_TF, d).astype(BF16)
    return wa, wu, wd


def kernel(x_prompt, x_sample, cache_k, cache_v, state_gla, state_conv, attn_w_qkv, attn_b_qkv, attn_sinks, attn_w_o, gla_w_in, gla_w_a1, gla_w_a2, gla_b_a, gla_norm_g, gla_w_o, ffn_w_up, ffn_conv_w, ffn_conv_b, ffn_w_down, ln_mix_g, ln_mix_b, ln_ffn_g, ln_ffn_b):
    bsz, seq, d = x_prompt.shape
    nsmp = x_sample.shape[0]
    xp = x_prompt
    xs = x_sample.reshape(nsmp, d)
    zeros = lambda n: jnp.zeros((1, n), F32)

    kp_l, vp_l, ks_l, vs_l, sp_l, ss_l, cp_l, cs_l = [], [], [], [], [], [], [], []
    for i in range(DEPTH):
        j = i // 2
        g_mix, b_mix = _row(ln_mix_g[i]), _row(ln_mix_b[i])
        if i % 2 == 0:
            wqkv = attn_w_qkv[j].astype(BF16)
            bqkv = _row(attn_b_qkv[j])
            wo = attn_w_o[j].astype(BF16)
            xp, kp, vp = _attn_prompt(xp, wqkv, bqkv, attn_sinks[j], wo, g_mix, b_mix)
            kv_shape = (bsz, WINDOW, N_KV_HEADS, HEAD_DIM)
            kp_l.append(kp.reshape(kv_shape))
            vp_l.append(vp.reshape(kv_shape))

            wbuf = cache_k.shape[2]
            qkv_s = _linear(xs, wqkv, bqkv)
            o_s, k_s, v_s = _attn_sample(qkv_s, cache_k[j].reshape(nsmp, wbuf, LANES),
                                         cache_v[j].reshape(nsmp, wbuf, LANES), attn_sinks[j])
            xs = _linear_ln(o_s, wo, xs, g_mix, b_mix)
            ks_l.append(k_s.reshape(cache_k.shape[1:]))
            vs_l.append(v_s.reshape(cache_v.shape[1:]))
        else:
            win = gla_w_in[j].astype(BF16)
            wa1 = jnp.pad(gla_w_a1[j], ((0, 0), (0, LANES - GLA_RANK))).astype(BF16)
            wa2 = jnp.pad(gla_w_a2[j], ((0, LANES - GLA_RANK), (0, 0))).astype(BF16)
            ba = _row(gla_b_a[j])
            ng = _row(gla_norm_g[j])
            wo = gla_w_o[j].astype(BF16)
            xp, sp = _gla_prompt(xp, win, wa1, wa2, ba, ng, wo, g_mix, b_mix)
            sp_l.append(sp)

            proj_s = _linear(xs, win, zeros(win.shape[1]))
            t1 = _linear(xs, wa1, zeros(LANES))
            z_s = _linear(t1, wa2, ba)
            o_s, s_s = _gla_sample(proj_s, z_s, state_gla[j], ng)
            xs = _linear_ln(o_s, wo, xs, g_mix, b_mix)
            ss_l.append(s_s)

        wa, wu, wd = _ffn_weights(ffn_w_up[i], ffn_w_down[i])
        cw, cb = ffn_conv_w[i], _row(ffn_conv_b[i])
        g_ffn, b_ffn = _row(ln_ffn_g[i]), _row(ln_ffn_b[i])
        xp2, cp = _ffn_prompt(xp.reshape(bsz * seq, d), seq, wa, wu, wd, cw, cb, g_ffn, b_ffn)
        xp = xp2.reshape(bsz, seq, d)
        cp_l.append(cp)
        xs, a_s = _ffn_sample(xs, state_conv[i, :, 0], state_conv[i, :, 1], wa, wu, wd, cw, cb,
                              g_ffn, b_ffn)
        cs_l.append(jnp.stack([state_conv[i, :, 1], a_s], axis=1))

    return (xp, xs.reshape(x_sample.shape), jnp.stack(kp_l), jnp.stack(vp_l), jnp.stack(ks_l),
            jnp.stack(vs_l), jnp.stack(sp_l), jnp.stack(ss_l), jnp.stack(cp_l), jnp.stack(cs_l))
```

```python
import functools

import jax
import jax.numpy as jnp
from jax import lax
from jax.experimental import pallas as pl
from jax.experimental.pallas import tpu as pltpu

F32 = jnp.float32
BF16 = jnp.bfloat16

D_MODEL = 1024
DEPTH = 2
PAST_LEN = 16384
N_HEADS = 16
N_KV_HEADS = 2
HEAD_DIM = 64
GROUP = N_HEADS // N_KV_HEADS
WINDOW = 128
ROPE_THETA = 10000.0
ATTN_SCALE = HEAD_DIM ** -0.5
GLA_HEADS = 4
GLA_DK = D_MODEL // 2
GLA_DV = D_MODEL
GLA_DKH = GLA_DK // GLA_HEADS
GLA_DVH = GLA_DV // GLA_HEADS
GLA_RANK = 16
GLA_TAU = 16.0
GLA_CHUNK = 64
D_FF = 2816
CONV_W = 3
ALPHA = (2 * DEPTH) ** 0.25
LN_EPS = 1e-5
NORM_EPS = 1e-6
NEG_INF = -1e30

LANES = 128
SUBLANES = 8
VMEM_LIMIT_BYTES = 56 * 1024 * 1024

ATTN_TQ = 512
FFN_TM = 512
FFN_TF = 256
FFN_ROW_SPLITS = 4
GLA_TM = 512
GLA_ROW_SPLITS = 4
SAMPLE_SEQ_BLOCK = 8


def _cparams(sem):
    return pltpu.CompilerParams(dimension_semantics=sem, vmem_limit_bytes=VMEM_LIMIT_BYTES)


def _resident(shape):
    nd = len(shape)
    return pl.BlockSpec(shape, lambda *_: (0,) * nd, pipeline_mode=pl.Buffered(1))


def _layer_norm(z, g, b):
    mu = jnp.mean(z, axis=-1, keepdims=True)
    zc = z - mu
    var = jnp.mean(zc * zc, axis=-1, keepdims=True)
    return zc * lax.rsqrt(var + LN_EPS) * g + b


def _gelu(c):
    return 0.5 * c * (1.0 + lax.erf(c * (2.0 ** -0.5)))


def _dot(a, b):
    return jnp.dot(a, b, preferred_element_type=F32)


def _dot_nt(a, b):
    return lax.dot_general(a, b, (((1,), (1,)), ((), ())), preferred_element_type=F32)


def _dot_tn(a, b):
    return lax.dot_general(a, b, (((0,), (0,)), ((), ())), preferred_element_type=F32)


def _rope_tables(pos):
    inv = 1.0 / (ROPE_THETA ** (jnp.arange(0, HEAD_DIM, 2, dtype=F32) / HEAD_DIM))
    ang = pos[:, None] * inv[None, :]
    cos, sin = jnp.cos(ang), jnp.sin(ang)
    zero = jnp.zeros_like(sin)
    cos_h = jnp.concatenate([cos, cos], -1)
    sa_h = jnp.concatenate([-sin, zero], -1)
    sb_h = jnp.concatenate([zero, sin], -1)
    two = lambda t: jnp.concatenate([t, t], -1)
    return two(cos_h), two(sa_h), two(sb_h)


def _rope_slab(s, cos, sa, sb):
    return (s * cos + pltpu.roll(s, LANES - HEAD_DIM // 2, 1) * sa
            + pltpu.roll(s, HEAD_DIM // 2, 1) * sb)


def _sink_softmax(s, sink):
    m = jnp.maximum(jnp.max(s, axis=-1, keepdims=True), sink)
    p = jnp.exp(s - m)
    den = jnp.sum(p, axis=-1, keepdims=True) + jnp.exp(sink - m)
    return p * (1.0 / den)


def _lane_split_pair(t):
    lo = lax.broadcasted_iota(jnp.int32, t.shape, 1) < HEAD_DIM
    tr = pltpu.roll(t, HEAD_DIM, 1)
    z = jnp.zeros_like(t)
    return (jnp.where(lo, t, z), jnp.where(lo, z, tr), jnp.where(lo, tr, z), jnp.where(lo, z, t))


def _attn_prompt_kernel(sinks_ref, x_ref, cos_ref, sa_ref, sb_ref, wqkv_ref, bqkv_ref, wo_ref,
                        g_ref, b_ref, y_ref, kn_ref, vn_ref, kprev_ref, vprev_ref):
    ti = pl.program_id(1)
    nq = N_HEADS * HEAD_DIM
    w = WINDOW

    @pl.when(ti == 0)
    def _():
        kprev_ref[...] = jnp.zeros_like(kprev_ref)
        vprev_ref[...] = jnp.zeros_like(vprev_ref)

    nblk = x_ref.shape[1] // w
    nslab = nq // LANES
    spg = nslab // N_KV_HEADS

    t = lax.broadcasted_iota(jnp.int32, (w, 2 * w), 0)
    s = lax.broadcasted_iota(jnp.int32, (w, 2 * w), 1)
    band = (s > t) & (s <= t + w)
    band_first = band & ((ti > 0) | (s >= w))

    def project(blk):
        rows = slice(blk * w, (blk + 1) * w)
        x = x_ref[0, rows, :]
        qkv = _dot(x.astype(BF16), wqkv_ref[...]) + bqkv_ref[...]
        cos, sa, sb = cos_ref[rows, :], sa_ref[rows, :], sb_ref[rows, :]
        k = _rope_slab(qkv[:, nq:nq + LANES], cos, sa, sb)
        v = qkv[:, nq + LANES:nq + 2 * LANES]
        q = [(_rope_slab(qkv[:, j * LANES:(j + 1) * LANES], cos, sa, sb) * ATTN_SCALE).astype(BF16)
             for j in range(nslab)]
        return dict(x=x, q=q, k=k, v=v, ks=_lane_split_pair(k.astype(BF16)),
                    vs=_lane_split_pair(v.astype(BF16)))

    def finish(blk, st, pv):
        slabs = []
        for g in range(N_KV_HEADS):
            og = pv[g, 0] + pv[g, 1]
            slabs += [og[i * w:(i + 1) * w] for i in range(spg)]
        o = jnp.concatenate(slabs, axis=1).astype(BF16)
        z = ALPHA * st["x"] + _dot(o, wo_ref[...])
        y_ref[0, blk * w:(blk + 1) * w, :] = _layer_norm(z, g_ref[...], b_ref[...])

    units = [(g, par) for g in range(N_KV_HEADS) for par in range(2)]
    prev = dict(ks=_lane_split_pair(kprev_ref[...].astype(BF16)),
                vs=_lane_split_pair(vprev_ref[...].astype(BF16)))
    stages = {-1: prev, 0: project(0)}
    pvs = {}
    for blk in range(nblk):
        if blk + 1 < nblk:
            stages[blk + 1] = project(blk + 1)
        st, pst = stages[blk], stages[blk - 1]
        mask = band_first if blk == 0 else band

        def scores(u):
            g, par = u
            qg = jnp.concatenate([st["q"][g * spg + i] for i in range(spg)], axis=0)
            kmat = jnp.concatenate([pst["ks"][2 * g + par], st["ks"][2 * g + par]], axis=0)
            return _dot_nt(qg, kmat)

        pv = {}
        sc_nxt = scores(units[0])
        for n, (g, par) in enumerate(units):
            sc = sc_nxt
            if n + 1 < len(units):
                sc_nxt = scores(units[n + 1])
            ps = []
            for i in range(spg):
                head = 2 * (g * spg + i) + par
                si = jnp.where(mask, sc[i * w:(i + 1) * w], NEG_INF)
                ps.append(_sink_softmax(si, sinks_ref[head]).astype(BF16))
            vmat = jnp.concatenate([pst["vs"][2 * g + par], st["vs"][2 * g + par]], axis=0)
            pv[g, par] = _dot(jnp.concatenate(ps, axis=0), vmat)
        pvs[blk] = pv
        if blk >= 1:
            finish(blk - 1, stages[blk - 1], pvs[blk - 1])
    finish(nblk - 1, stages[nblk - 1], pvs[nblk - 1])

    k_last, v_last = stages[nblk - 1]["k"], stages[nblk - 1]["v"]
    kprev_ref[...] = k_last
    vprev_ref[...] = v_last

    @pl.when(ti == pl.num_programs(1) - 1)
    def _():
        kn_ref[0] = k_last
        vn_ref[0] = v_last


def _attn_prompt(x, wqkv, bqkv, sinks, wo, g, b):
    bsz, seq, d = x.shape
    tq = min(ATTN_TQ, seq)
    nqkv = wqkv.shape[1]
    cos, sa, sb = _rope_tables(jnp.arange(seq, dtype=F32))
    tab = pl.BlockSpec((tq, LANES), lambda bi, ti: (ti, 0))
    kv_out = pl.BlockSpec((1, WINDOW, LANES), lambda bi, ti: (bi, 0, 0))
    return pl.pallas_call(
        _attn_prompt_kernel,
        grid=(bsz, seq // tq),
        in_specs=[
            pl.BlockSpec(memory_space=pltpu.SMEM),
            pl.BlockSpec((1, tq, d), lambda bi, ti: (bi, ti, 0)),
            tab, tab, tab,
            _resident((d, nqkv)), _resident((1, nqkv)), _resident((d, d)),
            _resident((1, d)), _resident((1, d)),
        ],
        out_specs=[pl.BlockSpec((1, tq, d), lambda bi, ti: (bi, ti, 0)), kv_out, kv_out],
        out_shape=[jax.ShapeDtypeStruct((bsz, seq, d), F32),
                   jax.ShapeDtypeStruct((bsz, WINDOW, LANES), F32),
                   jax.ShapeDtypeStruct((bsz, WINDOW, LANES), F32)],
        scratch_shapes=[pltpu.VMEM((WINDOW, LANES), F32), pltpu.VMEM((WINDOW, LANES), F32)],
        compiler_params=_cparams(("arbitrary", "arbitrary")),
        name="attn_prompt",
    )(sinks, x, cos, sa, sb, wqkv, bqkv, wo, g, b)


def _conv_gate(a, a1, a2, u, cw, cb):
    c = cb + cw[0:1] * a2 + cw[1:2] * a1 + cw[2:3] * a
    return (_gelu(c) * u).astype(BF16)


def _ffn_prompt_kernel(x_ref, wa_ref, wu_ref, wd_ref, cw_ref, cb_ref, g_ref, b_ref,
                       y_ref, cs_ref, carry_ref, h_ref, *, tiles_per_seq):
    i = pl.program_id(0)
    pos = i % tiles_per_seq

    @pl.when(pos == 0)
    def _():
        carry_ref[...] = jnp.zeros_like(carry_ref)

    x = x_ref[...]
    xb = x.astype(BF16)
    tm = x.shape[0]
    nchunk, _, tf = wa_ref.shape
    row = lax.broadcasted_iota(jnp.int32, (SUBLANES, tf), 0)
    for j in range(nchunk):
        sl = slice(j * tf, (j + 1) * tf)
        a = _dot(xb, wa_ref[j])
        u = _dot(xb, wu_ref[j])
        prev = carry_ref[:, sl]
        p1 = prev[SUBLANES - 1:SUBLANES]
        p2 = prev[SUBLANES - 2:SUBLANES - 1]
        a1 = pltpu.roll(a, 1, 0)
        a2 = pltpu.roll(a, 2, 0)
        a1_top = jnp.where(row == 0, p1, a1[:SUBLANES])
        a2_top = jnp.where(row == 0, p2, jnp.where(row == 1, p1, a2[:SUBLANES]))
        a1 = jnp.concatenate([a1_top, a1[SUBLANES:]], axis=0)
        a2 = jnp.concatenate([a2_top, a2[SUBLANES:]], axis=0)
        carry_ref[:, sl] = a[tm - SUBLANES:tm]
        h_ref[:, sl] = _conv_gate(a, a1, a2, u, cw_ref[:, sl], cb_ref[:, sl])

    rt = tm // FFN_ROW_SPLITS
    for r in range(FFN_ROW_SPLITS):
        rows = slice(r * rt, (r + 1) * rt)
        z = ALPHA * x[rows] + _dot(h_ref[rows, :], wd_ref[...])
        y_ref[rows, :] = _layer_norm(z, g_ref[...], b_ref[...])

    @pl.when(pos == tiles_per_seq - 1)
    def _():
        cs_ref[0] = carry_ref[SUBLANES - (CONV_W - 1):SUBLANES, :]


def _ffn_prompt(x, seq, wa, wu, wd, cw, cb, g, b):
    n, d = x.shape
    tm = min(FFN_TM, seq)
    tiles_per_seq = seq // tm
    nchunk, _, tf = wa.shape
    dff = nchunk * tf
    wd = wd.reshape(dff, d)
    return pl.pallas_call(
        functools.partial(_ffn_prompt_kernel, tiles_per_seq=tiles_per_seq),
        grid=(n // tm,),
        in_specs=[
            pl.BlockSpec((tm, d), lambda i: (i, 0)),
            _resident(wa.shape), _resident(wu.shape), _resident(wd.shape),
            _resident(cw.shape), _resident(cb.shape), _resident(g.shape), _resident(b.shape),
        ],
        out_specs=[pl.BlockSpec((tm, d), lambda i: (i, 0)),
                   pl.BlockSpec((1, CONV_W - 1, dff), lambda i: (i // tiles_per_seq, 0, 0))],
        out_shape=[jax.ShapeDtypeStruct((n, d), F32),
                   jax.ShapeDtypeStruct((n // seq, CONV_W - 1, dff), F32)],
        scratch_shapes=[pltpu.VMEM((SUBLANES, dff), F32), pltpu.VMEM((tm, dff), BF16)],
        compiler_params=_cparams(("arbitrary",)),
        name="ffn_prompt",
    )(x, wa, wu, wd, cw, cb, g, b)


def _ffn_sample_kernel(x_ref, s0_ref, s1_ref, wa_ref, wu_ref, wd_ref, cw_ref, cb_ref, g_ref, b_ref,
                       y_ref, a_ref, acc_ref):
    j = pl.program_id(0)
    x = x_ref[...]
    xb = x.astype(BF16)
    a = _dot(xb, wa_ref[0])
    u = _dot(xb, wu_ref[0])
    a_ref[...] = a
    h = _conv_gate(a, s1_ref[...], s0_ref[...], u, cw_ref[...], cb_ref[...])
    dn = _dot(h, wd_ref[0])

    @pl.when(j == 0)
    def _():
        acc_ref[...] = dn

    @pl.when(j > 0)
    def _():
        acc_ref[...] += dn

    @pl.when(j == pl.num_programs(0) - 1)
    def _():
        y_ref[...] = _layer_norm(ALPHA * x + acc_ref[...], g_ref[...], b_ref[...])


def _ffn_sample(x, s0, s1, wa, wu, wd, cw, cb, g, b):
    n, d = x.shape
    nchunk, _, tf = wa.shape
    dff = nchunk * tf
    col = pl.BlockSpec((n, tf), lambda j: (0, j))
    return pl.pallas_call(
        _ffn_sample_kernel,
        grid=(nchunk,),
        in_specs=[
            pl.BlockSpec((n, d), lambda j: (0, 0)), col, col,
            pl.BlockSpec((1, d, tf), lambda j: (j, 0, 0)),
            pl.BlockSpec((1, d, tf), lambda j: (j, 0, 0)),
            pl.BlockSpec((1, tf, d), lambda j: (j, 0, 0)),
            pl.BlockSpec((CONV_W, tf), lambda j: (0, j)),
            pl.BlockSpec((1, tf), lambda j: (0, j)),
            pl.BlockSpec((1, d), lambda j: (0, 0)), pl.BlockSpec((1, d), lambda j: (0, 0)),
        ],
        out_specs=[pl.BlockSpec((n, d), lambda j: (0, 0)), col],
        out_shape=[jax.ShapeDtypeStruct((n, d), F32), jax.ShapeDtypeStruct((n, dff), F32)],
        scratch_shapes=[pltpu.VMEM((n, d), F32)],
        compiler_params=_cparams(("arbitrary",)),
        name="ffn_sample",
    )(x, s0, s1, wa, wu, wd, cw, cb, g, b)


def _gla_decay_logits(xb, wa1_ref, wa2_ref, ba_ref):
    t1 = _dot(xb, wa1_ref[...])
    z = _dot(t1.astype(BF16), wa2_ref[...]) + ba_ref[...]
    return jax.nn.log_sigmoid(z) * (1.0 / GLA_TAU)


def _gla_out_gate(o, gate, ng):
    outs = []
    for h in range(GLA_HEADS):
        oh = o[:, h * GLA_DVH:(h + 1) * GLA_DVH]
        ms = jnp.mean(oh * oh, axis=-1, keepdims=True)
        outs.append(oh * lax.rsqrt(ms + NORM_EPS) * ng)
    on = jnp.concatenate(outs, axis=1)
    return (on * jax.nn.silu(gate)).astype(BF16)


def _split3(t):
    hi = t.astype(BF16)
    r1 = t - hi.astype(F32)
    mid = r1.astype(BF16)
    lo = (r1 - mid.astype(F32)).astype(BF16)
    return hi, mid, lo


def _gla_prompt_kernel(x_ref, win_ref, wa1_ref, wa2_ref, ba_ref, ng_ref, wo_ref, g_ref, b_ref,
                       y_ref, so_ref, s_ref, proj_ref, u_ref, o_ref):
    ti = pl.program_id(1)
    ck = GLA_CHUNK

    @pl.when(ti == 0)
    def _():
        s_ref[...] = jnp.zeros_like(s_ref)

    x = x_ref[0]
    xb = x.astype(BF16)
    tm = x.shape[0]
    nck = tm // ck
    la = _gla_decay_logits(xb, wa1_ref, wa2_ref, ba_ref)
    proj_ref[...] = _dot(xb, win_ref[...])
    la_parts = _split3(la)

    r = lax.broadcasted_iota(jnp.int32, (ck, ck), 0)
    c = lax.broadcasted_iota(jnp.int32, (ck, ck), 1)
    causal = r >= c
    tril = causal.astype(BF16)

    bcs = [sum(_dot(tril, part[ci * ck:(ci + 1) * ck]) for part in la_parts) for ci in range(nck)]

    q_in, vb, att, dec = {}, {}, {}, {}
    for ci in range(nck):
        rows = slice(ci * ck, (ci + 1) * ck)
        bc = bcs[ci]
        for h in range(GLA_HEADS):
            bh = bc[:, h * GLA_DKH:(h + 1) * GLA_DKH]
            q = proj_ref[rows, h * GLA_DKH:(h + 1) * GLA_DKH] * (GLA_DKH ** -0.5)
            k = proj_ref[rows, GLA_DK + h * GLA_DKH:GLA_DK + (h + 1) * GLA_DKH]
            v = proj_ref[rows, 2 * GLA_DK + h * GLA_DVH:2 * GLA_DK + (h + 1) * GLA_DVH].astype(BF16)
            qi = (q * jnp.exp(bh)).astype(BF16)
            k_in = (k * jnp.exp(-bh)).astype(BF16)
            bl = bh[ck - 1:ck]
            kd = (k * jnp.exp(bl - bh)).astype(BF16)
            att[ci, h] = _dot_nt(qi, k_in)
            u_ref[ci, h] = _dot_tn(kd, v)
            q_in[ci, h], vb[ci, h], dec[ci, h] = qi, v, jnp.exp(bl)

    state = [s_ref[h] for h in range(GLA_HEADS)]
    for ci in range(nck):
        rows = slice(ci * ck, (ci + 1) * ck)
        for h in range(GLA_HEADS):
            am = jnp.where(causal, att[ci, h], 0.0).astype(BF16)
            o_ref[rows, h * GLA_DVH:(h + 1) * GLA_DVH] = (
                _dot(am, vb[ci, h]) + _dot(q_in[ci, h], state[h].astype(BF16)))
            dcol = jnp.transpose(jnp.broadcast_to(dec[ci, h], (GLA_DKH, GLA_DKH)))
            dfull = jnp.concatenate([dcol] * (GLA_DVH // GLA_DKH), axis=1)
            state[h] = dfull * state[h] + u_ref[ci, h]
    for h in range(GLA_HEADS):
        s_ref[h] = state[h]

    rt = tm // GLA_ROW_SPLITS
    for rs in range(GLA_ROW_SPLITS):
        rows = slice(rs * rt, (rs + 1) * rt)
        og = _gla_out_gate(o_ref[rows, :], proj_ref[rows, 2 * GLA_DK + GLA_DV:], ng_ref[...])
        z = ALPHA * x[rows] + _dot(og, wo_ref[...])
        y_ref[0, rows, :] = _layer_norm(z, g_ref[...], b_ref[...])

    @pl.when(ti == pl.num_programs(1) - 1)
    def _():
        so_ref[0] = s_ref[...]


def _gla_prompt(x, win, wa1, wa2, ba, ng, wo, g, b):
    bsz, seq, d = x.shape
    tm = min(GLA_TM, seq)
    nin = win.shape[1]
    st = (GLA_HEADS, GLA_DKH, GLA_DVH)
    return pl.pallas_call(
        _gla_prompt_kernel,
        grid=(bsz, seq // tm),
        in_specs=[
            pl.BlockSpec((1, tm, d), lambda bi, ti: (bi, ti, 0)),
            _resident(win.shape), _resident(wa1.shape), _resident(wa2.shape), _resident(ba.shape),
            _resident(ng.shape), _resident(wo.shape), _resident(g.shape), _resident(b.shape),
        ],
        out_specs=[pl.BlockSpec((1, tm, d), lambda bi, ti: (bi, ti, 0)),
                   pl.BlockSpec((1,) + st, lambda bi, ti: (bi, 0, 0, 0))],
        out_shape=[jax.ShapeDtypeStruct((bsz, seq, d), F32),
                   jax.ShapeDtypeStruct((bsz,) + st, F32)],
        scratch_shapes=[pltpu.VMEM(st, F32), pltpu.VMEM((tm, nin), F32),
                        pltpu.VMEM((tm // GLA_CHUNK,) + st, F32), pltpu.VMEM((tm, GLA_DV), F32)],
        compiler_params=_cparams(("arbitrary", "arbitrary")),
        name="gla_prompt",
    )(x, win, wa1, wa2, ba, ng, wo, g, b)


def _linear_kernel(x_ref, w_ref, b_ref, o_ref):
    o_ref[...] = _dot(x_ref[...].astype(BF16), w_ref[...]) + b_ref[...]


def _linear(x, w, bias):
    n, kdim = x.shape
    nout = w.shape[1]
    tn = next(t for t in (512, 256, 128) if nout % t == 0)
    return pl.pallas_call(
        _linear_kernel,
        grid=(nout // tn,),
        in_specs=[pl.BlockSpec((n, kdim), lambda j: (0, 0)),
                  pl.BlockSpec((kdim, tn), lambda j: (0, j)),
                  pl.BlockSpec((1, tn), lambda j: (0, j))],
        out_specs=pl.BlockSpec((n, tn), lambda j: (0, j)),
        out_shape=jax.ShapeDtypeStruct((n, nout), F32),
        compiler_params=_cparams(("arbitrary",)),
        name="sample_linear",
    )(x, w, bias)


def _linear_ln_kernel(h_ref, w_ref, r_ref, g_ref, b_ref, o_ref):
    z = ALPHA * r_ref[...] + _dot(h_ref[...].astype(BF16), w_ref[...])
    o_ref[...] = _layer_norm(z, g_ref[...], b_ref[...])


def _linear_ln(h, w, resid, g, b):
    n, d = resid.shape
    full = lambda a: pl.BlockSpec(a.shape, lambda i: (0,) * a.ndim)
    return pl.pallas_call(
        _linear_ln_kernel,
        grid=(1,),
        in_specs=[full(h), full(w), full(resid), full(g), full(b)],
        out_specs=pl.BlockSpec((n, d), lambda i: (0, 0)),
        out_shape=jax.ShapeDtypeStruct((n, d), F32),
        compiler_params=_cparams(("arbitrary",)),
        name="sample_linear_ln",
    )(h, w, resid, g, b)


def _attn_sample_kernel(qkv_ref, ck_ref, cv_ref, cos_ref, sa_ref, sb_ref, sink_ref,
                        o_ref, kn_ref, vn_ref):
    nq = N_HEADS * HEAD_DIM
    wbuf = ck_ref.shape[1]
    cos, sa, sb = cos_ref[...], sa_ref[...], sb_ref[...]
    qkv = qkv_ref[...]
    q_all = jnp.concatenate(
        [_rope_slab(qkv[:, j * LANES:(j + 1) * LANES], cos, sa, sb) for j in range(nq // LANES)],
        axis=1) * ATTN_SCALE
    k_all = _rope_slab(qkv[:, nq:nq + LANES], cos, sa, sb)
    v_all = qkv[:, nq + LANES:nq + 2 * LANES]

    head_of_lane = lax.broadcasted_iota(jnp.int32, (N_HEADS, nq), 1) // HEAD_DIM
    head_of_row = lax.broadcasted_iota(jnp.int32, (N_HEADS, nq), 0)
    own = head_of_lane == head_of_row
    lo = lax.broadcasted_iota(jnp.int32, (wbuf, LANES), 1) < HEAD_DIM
    last = lax.broadcasted_iota(jnp.int32, (wbuf, LANES), 0) == wbuf - 1
    reps = N_HEADS // N_KV_HEADS * HEAD_DIM // LANES

    def widen(t):
        tr = pltpu.roll(t, HEAD_DIM, 1)
        t0 = jnp.where(lo, t, tr)
        t1 = jnp.where(lo, tr, t)
        return jnp.concatenate([t0] * reps + [t1] * reps, axis=1).astype(BF16)

    sink = sink_ref[...]
    rows = []
    for s in range(qkv.shape[0]):
        k_new = jnp.where(last, k_all[s:s + 1], pltpu.roll(ck_ref[s], wbuf - 1, 0))
        v_new = jnp.where(last, v_all[s:s + 1], pltpu.roll(cv_ref[s], wbuf - 1, 0))
        kn_ref[s] = k_new
        vn_ref[s] = v_new
        qm = jnp.where(own, q_all[s:s + 1], 0.0).astype(BF16)
        sc = _dot_nt(qm, widen(k_new))
        p = _sink_softmax(sc, sink).astype(BF16)
        o16 = _dot(p, widen(v_new))
        rows.append(jnp.sum(jnp.where(own, o16, 0.0), axis=0, keepdims=True))
    o_ref[...] = jnp.concatenate(rows, axis=0)


def _attn_sample(qkv, ck, cv, sinks):
    n = qkv.shape[0]
    wbuf = ck.shape[1]
    sb_ = SAMPLE_SEQ_BLOCK
    nq = N_HEADS * HEAD_DIM
    cos, sa, sb = _rope_tables(jnp.full((1,), PAST_LEN, dtype=F32))
    one = lambda a: pl.BlockSpec(a.shape, lambda i: (0,) * a.ndim)
    sink_col = sinks.reshape(N_HEADS, 1)
    cache = pl.BlockSpec((sb_, wbuf, LANES), lambda i: (i, 0, 0))
    return pl.pallas_call(
        _attn_sample_kernel,
        grid=(n // sb_,),
        in_specs=[pl.BlockSpec((sb_, qkv.shape[1]), lambda i: (i, 0)), cache, cache,
                  one(cos), one(sa), one(sb), one(sink_col)],
        out_specs=[pl.BlockSpec((sb_, nq), lambda i: (i, 0)), cache, cache],
        out_shape=[jax.ShapeDtypeStruct((n, nq), F32),
                   jax.ShapeDtypeStruct(ck.shape, F32), jax.ShapeDtypeStruct(cv.shape, F32)],
        compiler_params=_cparams(("arbitrary",)),
        name="attn_sample",
    )(qkv, ck, cv, cos, sa, sb, sink_col)


def _gla_sample_kernel(proj_ref, z_ref, s0_ref, ng_ref, o_ref, sn_ref, oacc_ref):
    proj = proj_ref[...]
    la = jax.nn.log_sigmoid(z_ref[...]) * (1.0 / GLA_TAU)
    nseq = proj.shape[0]
    pad = jnp.zeros((LANES - nseq, GLA_DKH), F32)

    def cols(t):
        return jnp.transpose(jnp.concatenate([t, pad], axis=0))

    for h in range(GLA_HEADS):
        b = la[:, h * GLA_DKH:(h + 1) * GLA_DKH]
        q = proj[:, h * GLA_DKH:(h + 1) * GLA_DKH] * (GLA_DKH ** -0.5)
        k = proj[:, GLA_DK + h * GLA_DKH:GLA_DK + (h + 1) * GLA_DKH]
        v = proj[:, 2 * GLA_DK + h * GLA_DVH:2 * GLA_DK + (h + 1) * GLA_DVH]
        eb = jnp.exp(b)
        q_in = q * eb
        k_in = k * jnp.exp(-b)
        att = jnp.sum(q_in * k_in, axis=-1, keepdims=True)
        kd = k * jnp.exp(b - b)
        qc, kc, ec = cols(q_in), cols(kd), cols(eb)
        for s in range(nseq):
            s_old = s0_ref[s, h]
            vrow = v[s:s + 1]
            o_row = att[s:s + 1] * vrow + jnp.sum(qc[:, s:s + 1] * s_old, axis=0, keepdims=True)
            oacc_ref[s:s + 1, h * GLA_DVH:(h + 1) * GLA_DVH] = o_row
            sn_ref[s, h] = ec[:, s:s + 1] * s_old + kc[:, s:s + 1] * vrow
    og = _gla_out_gate(oacc_ref[...], proj[:, 2 * GLA_DK + GLA_DV:], ng_ref[...])
    o_ref[...] = og.astype(F32)


def _gla_sample(proj, z, s0, ng):
    n = proj.shape[0]
    sb_ = SAMPLE_SEQ_BLOCK
    st = pl.BlockSpec((sb_,) + s0.shape[1:], lambda i: (i, 0, 0, 0))
    return pl.pallas_call(
        _gla_sample_kernel,
        grid=(n // sb_,),
        in_specs=[pl.BlockSpec((sb_, proj.shape[1]), lambda i: (i, 0)),
                  pl.BlockSpec((sb_, z.shape[1]), lambda i: (i, 0)), st,
                  pl.BlockSpec(ng.shape, lambda i: (0, 0))],
        out_specs=[pl.BlockSpec((sb_, GLA_DV), lambda i: (i, 0)), st],
        out_shape=[jax.ShapeDtypeStruct((n, GLA_DV), F32), jax.ShapeDtypeStruct(s0.shape, F32)],
        scratch_shapes=[pltpu.VMEM((sb_, GLA_DV), F32)],
        compiler_params=_cparams(("arbitrary",)),
        name="gla_sample",
    )(proj, z, s0, ng)


def _row(v):
    return v.reshape(1, -1)


def _ffn_weights(w_up, w_down):
    d = w_up.shape[0]
    nchunk = D_FF // FFN_TF
    wa = w_up[:, :D_FF].reshape(d, nchunk, FFN_TF).transpose(1, 0, 2).astype(BF16)
    wu = w_up[:, D_FF:].reshape(d, nchunk, FFN_TF).transpose(1, 0, 2).astype(BF16)
    wd = w_down.reshape(nchunk, FFN_TF, d).astype(BF16)
    return wa, wu, wd


def kernel(x_prompt, x_sample, cache_k, cache_v, state_gla, state_conv, attn_w_qkv, attn_b_qkv, attn_sinks, attn_w_o, gla_w_in, gla_w_a1, gla_w_a2, gla_b_a, gla_norm_g, gla_w_o, ffn_w_up, ffn_conv_w, ffn_conv_b, ffn_w_down, ln_mix_g, ln_mix_b, ln_ffn_g, ln_ffn_b):
    bsz, seq, d = x_prompt.shape
    nsmp = x_sample.shape[0]
    xp = x_prompt
    xs = x_sample.reshape(nsmp, d)
    zeros = lambda n: jnp.zeros((1, n), F32)

    kp_l, vp_l, ks_l, vs_l, sp_l, ss_l, cp_l, cs_l = [], [], [], [], [], [], [], []
    for i in range(DEPTH):
        j = i // 2
        g_mix, b_mix = _row(ln_mix_g[i]), _row(ln_mix_b[i])
        if i % 2 == 0:
            wqkv = attn_w_qkv[j].astype(BF16)
            bqkv = _row(attn_b_qkv[j])
            wo = attn_w_o[j].astype(BF16)
            xp, kp, vp = _attn_prompt(xp, wqkv, bqkv, attn_sinks[j], wo, g_mix, b_mix)
            kv_shape = (bsz, WINDOW, N_KV_HEADS, HEAD_DIM)
            kp_l.append(kp.reshape(kv_shape))
            vp_l.append(vp.reshape(kv_shape))

            wbuf = cache_k.shape[2]
            qkv_s = _linear(xs, wqkv, bqkv)
            o_s, k_s, v_s = _attn_sample(qkv_s, cache_k[j].reshape(nsmp, wbuf, LANES),
                                         cache_v[j].reshape(nsmp, wbuf, LANES), attn_sinks[j])
            xs = _linear_ln(o_s, wo, xs, g_mix, b_mix)
            ks_l.append(k_s.reshape(cache_k.shape[1:]))
            vs_l.append(v_s.reshape(cache_v.shape[1:]))
        else:
            win = gla_w_in[j].astype(BF16)
            wa1 = jnp.pad(gla_w_a1[j], ((0, 0), (0, LANES - GLA_RANK))).astype(BF16)
            wa2 = jnp.pad(gla_w_a2[j], ((0, LANES - GLA_RANK), (0, 0))).astype(BF16)
            ba = _row(gla_b_a[j])
            ng = _row(gla_norm_g[j])
            wo = gla_w_o[j].astype(BF16)
            xp, sp = _gla_prompt(xp, win, wa1, wa2, ba, ng, wo, g_mix, b_mix)
            sp_l.append(sp)

            proj_s = _linear(xs, win, zeros(win.shape[1]))
            t1 = _linear(xs, wa1, zeros(LANES))
            z_s = _linear(t1, wa2, ba)
            o_s, s_s = _gla_sample(proj_s, z_s, state_gla[j], ng)
            xs = _linear_ln(o_s, wo, xs, g_mix, b_mix)
            ss_l.append(s_s)

        wa, wu, wd = _ffn_weights(ffn_w_up[i], ffn_w_down[i])
        cw, cb = ffn_conv_w[i], _row(ffn_conv_b[i])
        g_ffn, b_ffn = _row(ln_ffn_g[i]), _row(ln_ffn_b[i])
        xp2, cp = _ffn_prompt(xp.reshape(bsz * seq, d), seq, wa, wu, wd, cw, cb, g_ffn, b_ffn)
        xp = xp2.reshape(bsz, seq, d)
        cp_l.append(cp)
        xs, a_s = _ffn_sample(xs, state_conv[i, :, 0], state_conv[i, :, 1], wa, wu, wd, cw, cb,
                              g_ffn, b_ffn)
        cs_l.append(jnp.stack([state_conv[i, :, 1], a_s], axis=1))

    return (xp, xs.reshape(x_sample.shape), jnp.stack(kp_l), jnp.stack(vp_l), jnp.stack(ks_l),
            jnp.stack(vs_l), jnp.stack(sp_l), jnp.stack(ss_l), jnp.stack(cp_l), jnp.stack(cs_l))
```

```python
import functools

import jax
import jax.numpy as jnp
from jax import lax
from jax.experimental import pallas as pl
from jax.experimental.pallas import tpu as pltpu

F32 = jnp.float32
BF16 = jnp.bfloat16

D_MODEL = 1024
DEPTH = 2
PAST_LEN = 16384
N_HEADS = 16
N_KV_HEADS = 2
HEAD_DIM = 64
GROUP = N_HEADS // N_KV_HEADS
WINDOW = 128
ROPE_THETA = 10000.0
ATTN_SCALE = HEAD_DIM ** -0.5
GLA_HEADS = 4
GLA_DK = D_MODEL // 2
GLA_DV = D_MODEL
GLA_DKH = GLA_DK // GLA_HEADS
GLA_DVH = GLA_DV // GLA_HEADS
GLA_RANK = 16
GLA_TAU = 16.0
GLA_CHUNK = 64
D_FF = 2816
CONV_W = 3
ALPHA = (2 * DEPTH) ** 0.25
LN_EPS = 1e-5
NORM_EPS = 1e-6
NEG_INF = -1e30

LANES = 128
SUBLANES = 8
VMEM_LIMIT_BYTES = 56 * 1024 * 1024

ATTN_TQ = 1024
FFN_TM = 1024
FFN_TF = 256
FFN_ROW_SPLITS = 4
GLA_TM = 512
GLA_ROW_SPLITS = 4
SAMPLE_SEQ_BLOCK = 8


def _cparams(sem):
    return pltpu.CompilerParams(dimension_semantics=sem, vmem_limit_bytes=VMEM_LIMIT_BYTES)


def _resident(shape):
    nd = len(shape)
    return pl.BlockSpec(shape, lambda *_: (0,) * nd, pipeline_mode=pl.Buffered(1))


def _layer_norm(z, g, b):
    mu = jnp.mean(z, axis=-1, keepdims=True)
    zc = z - mu
    var = jnp.mean(zc * zc, axis=-1, keepdims=True)
    return zc * lax.rsqrt(var + LN_EPS) * g + b


def _gelu(c):
    return 0.5 * c * (1.0 + lax.erf(c * (2.0 ** -0.5)))


def _dot(a, b):
    return jnp.dot(a, b, preferred_element_type=F32)


def _dot_nt(a, b):
    return lax.dot_general(a, b, (((1,), (1,)), ((), ())), preferred_element_type=F32)


def _dot_tn(a, b):
    return lax.dot_general(a, b, (((0,), (0,)), ((), ())), preferred_element_type=F32)


def _rope_tables(pos):
    inv = 1.0 / (ROPE_THETA ** (jnp.arange(0, HEAD_DIM, 2, dtype=F32) / HEAD_DIM))
    ang = pos[:, None] * inv[None, :]
    cos, sin = jnp.cos(ang), jnp.sin(ang)
    zero = jnp.zeros_like(sin)
    cos_h = jnp.concatenate([cos, cos], -1)
    sa_h = jnp.concatenate([-sin, zero], -1)
    sb_h = jnp.concatenate([zero, sin], -1)
    two = lambda t: jnp.concatenate([t, t], -1)
    return two(cos_h), two(sa_h), two(sb_h)


def _rope_slab(s, cos, sa, sb):
    return (s * cos + pltpu.roll(s, LANES - HEAD_DIM // 2, 1) * sa
            + pltpu.roll(s, HEAD_DIM // 2, 1) * sb)


def _sink_softmax(s, sink):
    m = jnp.maximum(jnp.max(s, axis=-1, keepdims=True), sink)
    p = jnp.exp(s - m)
    den = jnp.sum(p, axis=-1, keepdims=True) + jnp.exp(sink - m)
    return p * (1.0 / den)


def _lane_split_pair(t):
    lo = lax.broadcasted_iota(jnp.int32, t.shape, 1) < HEAD_DIM
    tr = pltpu.roll(t, HEAD_DIM, 1)
    z = jnp.zeros_like(t)
    return (jnp.where(lo, t, z), jnp.where(lo, z, tr), jnp.where(lo, tr, z), jnp.where(lo, z, t))


def _attn_prompt_kernel(sinks_ref, x_ref, cos_ref, sa_ref, sb_ref, wqkv_ref, bqkv_ref, wo_ref,
                        g_ref, b_ref, y_ref, kn_ref, vn_ref, kprev_ref, vprev_ref):
    ti = pl.program_id(1)
    nq = N_HEADS * HEAD_DIM
    w = WINDOW

    @pl.when(ti == 0)
    def _():
        kprev_ref[...] = jnp.zeros_like(kprev_ref)
        vprev_ref[...] = jnp.zeros_like(vprev_ref)

    nblk = x_ref.shape[1] // w
    nslab = nq // LANES
    spg = nslab // N_KV_HEADS

    t = lax.broadcasted_iota(jnp.int32, (w, 2 * w), 0)
    s = lax.broadcasted_iota(jnp.int32, (w, 2 * w), 1)
    band = (s > t) & (s <= t + w)
    band_first = band & ((ti > 0) | (s >= w))

    def project(blk):
        rows = slice(blk * w, (blk + 1) * w)
        x = x_ref[0, rows, :]
        qkv = _dot(x.astype(BF16), wqkv_ref[...]) + bqkv_ref[...]
        cos, sa, sb = cos_ref[rows, :], sa_ref[rows, :], sb_ref[rows, :]
        k = _rope_slab(qkv[:, nq:nq + LANES], cos, sa, sb)
        v = qkv[:, nq + LANES:nq + 2 * LANES]
        q = [(_rope_slab(qkv[:, j * LANES:(j + 1) * LANES], cos, sa, sb) * ATTN_SCALE).astype(BF16)
             for j in range(nslab)]
        return dict(x=x, q=q, k=k, v=v, ks=_lane_split_pair(k.astype(BF16)),
                    vs=_lane_split_pair(v.astype(BF16)))

    def finish(blk, st, pv):
        slabs = []
        for g in range(N_KV_HEADS):
            og = pv[g, 0] + pv[g, 1]
            slabs += [og[i * w:(i + 1) * w] for i in range(spg)]
        o = jnp.concatenate(slabs, axis=1).astype(BF16)
        z = ALPHA * st["x"] + _dot(o, wo_ref[...])
        y_ref[0, blk * w:(blk + 1) * w, :] = _layer_norm(z, g_ref[...], b_ref[...])

    units = [(g, par) for g in range(N_KV_HEADS) for par in range(2)]
    prev = dict(ks=_lane_split_pair(kprev_ref[...].astype(BF16)),
                vs=_lane_split_pair(vprev_ref[...].astype(BF16)))
    stages = {-1: prev, 0: project(0)}
    pvs = {}
    for blk in range(nblk):
        if blk + 1 < nblk:
            stages[blk + 1] = project(blk + 1)
        st, pst = stages[blk], stages[blk - 1]
        mask = band_first if blk == 0 else band

        def scores(u):
            g, par = u
            qg = jnp.concatenate([st["q"][g * spg + i] for i in range(spg)], axis=0)
            kmat = jnp.concatenate([pst["ks"][2 * g + par], st["ks"][2 * g + par]], axis=0)
            return _dot_nt(qg, kmat)

        pv = {}
        sc_nxt = scores(units[0])
        for n, (g, par) in enumerate(units):
            sc = sc_nxt
            if n + 1 < len(units):
                sc_nxt = scores(units[n + 1])
            ps = []
            for i in range(spg):
                head = 2 * (g * spg + i) + par
                si = jnp.where(mask, sc[i * w:(i + 1) * w], NEG_INF)
                ps.append(_sink_softmax(si, sinks_ref[head]).astype(BF16))
            vmat = jnp.concatenate([pst["vs"][2 * g + par], st["vs"][2 * g + par]], axis=0)
            pv[g, par] = _dot(jnp.concatenate(ps, axis=0), vmat)
        pvs[blk] = pv
        if blk >= 1:
            finish(blk - 1, stages[blk - 1], pvs[blk - 1])
    finish(nblk - 1, stages[nblk - 1], pvs[nblk - 1])

    k_last, v_last = stages[nblk - 1]["k"], stages[nblk - 1]["v"]
    kprev_ref[...] = k_last
    vprev_ref[...] = v_last

    @pl.when(ti == pl.num_programs(1) - 1)
    def _():
        kn_ref[0] = k_last
        vn_ref[0] = v_last


def _attn_prompt(x, wqkv, bqkv, sinks, wo, g, b):
    bsz, seq, d = x.shape
    tq = min(ATTN_TQ, seq)
    nqkv = wqkv.shape[1]
    cos, sa, sb = _rope_tables(jnp.arange(seq, dtype=F32))
    tab = pl.BlockSpec((tq, LANES), lambda bi, ti: (ti, 0))
    kv_out = pl.BlockSpec((1, WINDOW, LANES), lambda bi, ti: (bi, 0, 0))
    return pl.pallas_call(
        _attn_prompt_kernel,
        grid=(bsz, seq // tq),
        in_specs=[
            pl.BlockSpec(memory_space=pltpu.SMEM),
            pl.BlockSpec((1, tq, d), lambda bi, ti: (bi, ti, 0)),
            tab, tab, tab,
            _resident((d, nqkv)), _resident((1, nqkv)), _resident((d, d)),
            _resident((1, d)), _resident((1, d)),
        ],
        out_specs=[pl.BlockSpec((1, tq, d), lambda bi, ti: (bi, ti, 0)), kv_out, kv_out],
        out_shape=[jax.ShapeDtypeStruct((bsz, seq, d), F32),
                   jax.ShapeDtypeStruct((bsz, WINDOW, LANES), F32),
                   jax.ShapeDtypeStruct((bsz, WINDOW, LANES), F32)],
        scratch_shapes=[pltpu.VMEM((WINDOW, LANES), F32), pltpu.VMEM((WINDOW, LANES), F32)],
        compiler_params=_cparams(("arbitrary", "arbitrary")),
        name="attn_prompt",
    )(sinks, x, cos, sa, sb, wqkv, bqkv, wo, g, b)


def _conv_gate(a, a1, a2, u, cw, cb):
    c = cb + cw[0:1] * a2 + cw[1:2] * a1 + cw[2:3] * a
    return (_gelu(c) * u).astype(BF16)


def _ffn_prompt_kernel(x_ref, wup_ref, wd_ref, cw_ref, cb_ref, g_ref, b_ref,
                       y_ref, cs_ref, carry_ref, h_ref, *, tiles_per_seq):
    i = pl.program_id(0)
    pos = i % tiles_per_seq

    @pl.when(pos == 0)
    def _():
        carry_ref[...] = jnp.zeros_like(carry_ref)

    x = x_ref[...]
    xb = x.astype(BF16)
    tm = x.shape[0]
    dff = wd_ref.shape[0]
    tf = FFN_TF
    row = lax.broadcasted_iota(jnp.int32, (SUBLANES, tf), 0)
    for j in range(dff // tf):
        sl = slice(j * tf, (j + 1) * tf)
        a = _dot(xb, wup_ref[:, sl])
        u = _dot(xb, wup_ref[:, dff + j * tf:dff + (j + 1) * tf])
        prev = carry_ref[:, sl]
        p1 = prev[SUBLANES - 1:SUBLANES]
        p2 = prev[SUBLANES - 2:SUBLANES - 1]
        a1 = pltpu.roll(a, 1, 0)
        a2 = pltpu.roll(a, 2, 0)
        a1_top = jnp.where(row == 0, p1, a1[:SUBLANES])
        a2_top = jnp.where(row == 0, p2, jnp.where(row == 1, p1, a2[:SUBLANES]))
        a1 = jnp.concatenate([a1_top, a1[SUBLANES:]], axis=0)
        a2 = jnp.concatenate([a2_top, a2[SUBLANES:]], axis=0)
        carry_ref[:, sl] = a[tm - SUBLANES:tm]
        h_ref[:, sl] = _conv_gate(a, a1, a2, u, cw_ref[:, sl], cb_ref[:, sl])

    rt = tm // FFN_ROW_SPLITS
    for r in range(FFN_ROW_SPLITS):
        rows = slice(r * rt, (r + 1) * rt)
        z = ALPHA * x[rows] + _dot(h_ref[rows, :], wd_ref[...])
        y_ref[rows, :] = _layer_norm(z, g_ref[...], b_ref[...])

    @pl.when(pos == tiles_per_seq - 1)
    def _():
        cs_ref[0] = carry_ref[SUBLANES - (CONV_W - 1):SUBLANES, :]


def _ffn_prompt(x, seq, wup, wd, cw, cb, g, b):
    n, d = x.shape
    tm = min(FFN_TM, seq)
    tiles_per_seq = seq // tm
    dff = wd.shape[0]
    return pl.pallas_call(
        functools.partial(_ffn_prompt_kernel, tiles_per_seq=tiles_per_seq),
        grid=(n // tm,),
        in_specs=[
            pl.BlockSpec((tm, d), lambda i: (i, 0)),
            _resident(wup.shape), _resident(wd.shape),
            _resident(cw.shape), _resident(cb.shape), _resident(g.shape), _resident(b.shape),
        ],
        out_specs=[pl.BlockSpec((tm, d), lambda i: (i, 0)),
                   pl.BlockSpec((1, CONV_W - 1, dff), lambda i: (i // tiles_per_seq, 0, 0))],
        out_shape=[jax.ShapeDtypeStruct((n, d), F32),
                   jax.ShapeDtypeStruct((n // seq, CONV_W - 1, dff), F32)],
        scratch_shapes=[pltpu.VMEM((SUBLANES, dff), F32), pltpu.VMEM((tm, dff), BF16)],
        compiler_params=_cparams(("arbitrary",)),
        name="ffn_prompt",
    )(x, wup, wd, cw, cb, g, b)


def _ffn_sample_kernel(x_ref, s0_ref, s1_ref, wa_ref, wu_ref, wd_ref, cw_ref, cb_ref, g_ref, b_ref,
                       y_ref, a_ref, acc_ref):
    j = pl.program_id(0)
    x = x_ref[...]
    xb = x.astype(BF16)
    a = _dot(xb, wa_ref[...])
    u = _dot(xb, wu_ref[...])
    a_ref[...] = a
    h = _conv_gate(a, s1_ref[...], s0_ref[...], u, cw_ref[...], cb_ref[...])
    dn = _dot(h, wd_ref[...])

    @pl.when(j == 0)
    def _():
        acc_ref[...] = dn

    @pl.when(j > 0)
    def _():
        acc_ref[...] += dn

    @pl.when(j == pl.num_programs(0) - 1)
    def _():
        y_ref[...] = _layer_norm(ALPHA * x + acc_ref[...], g_ref[...], b_ref[...])


def _ffn_sample(x, s0, s1, wup, wd, cw, cb, g, b):
    n, d = x.shape
    dff = wd.shape[0]
    tf = FFN_TF
    nchunk = dff // tf
    col = pl.BlockSpec((n, tf), lambda j: (0, j))
    return pl.pallas_call(
        _ffn_sample_kernel,
        grid=(nchunk,),
        in_specs=[
            pl.BlockSpec((n, d), lambda j: (0, 0)), col, col,
            pl.BlockSpec((d, tf), lambda j: (0, j)),
            pl.BlockSpec((d, tf), lambda j: (0, j + nchunk)),
            pl.BlockSpec((tf, d), lambda j: (j, 0)),
            pl.BlockSpec((CONV_W, tf), lambda j: (0, j)),
            pl.BlockSpec((1, tf), lambda j: (0, j)),
            pl.BlockSpec((1, d), lambda j: (0, 0)), pl.BlockSpec((1, d), lambda j: (0, 0)),
        ],
        out_specs=[pl.BlockSpec((n, d), lambda j: (0, 0)), col],
        out_shape=[jax.ShapeDtypeStruct((n, d), F32), jax.ShapeDtypeStruct((n, dff), F32)],
        scratch_shapes=[pltpu.VMEM((n, d), F32)],
        compiler_params=_cparams(("arbitrary",)),
        name="ffn_sample",
    )(x, s0, s1, wup, wup, wd, cw, cb, g, b)


def _gla_decay_logits(xb, wa1_ref, wa2_ref, ba_ref):
    t1 = _dot(xb, wa1_ref[...])
    z = _dot(t1.astype(BF16), wa2_ref[...]) + ba_ref[...]
    return jax.nn.log_sigmoid(z) * (1.0 / GLA_TAU)


def _gla_out_gate(o, gate, ng):
    outs = []
    for h in range(GLA_HEADS):
        oh = o[:, h * GLA_DVH:(h + 1) * GLA_DVH]
        ms = jnp.mean(oh * oh, axis=-1, keepdims=True)
        outs.append(oh * lax.rsqrt(ms + NORM_EPS) * ng)
    on = jnp.concatenate(outs, axis=1)
    return (on * jax.nn.silu(gate)).astype(BF16)


def _split3(t):
    hi = t.astype(BF16)
    r1 = t - hi.astype(F32)
    mid = r1.astype(BF16)
    lo = (r1 - mid.astype(F32)).astype(BF16)
    return hi, mid, lo


def _gla_prompt_kernel(x_ref, win_ref, wa1_ref, wa2_ref, ba_ref, ng_ref, wo_ref, g_ref, b_ref,
                       y_ref, so_ref, s_ref, proj_ref, u_ref, o_ref):
    ti = pl.program_id(1)
    ck = GLA_CHUNK

    @pl.when(ti == 0)
    def _():
        s_ref[...] = jnp.zeros_like(s_ref)

    x = x_ref[0]
    xb = x.astype(BF16)
    tm = x.shape[0]
    nck = tm // ck
    la = _gla_decay_logits(xb, wa1_ref, wa2_ref, ba_ref)
    proj_ref[...] = _dot(xb, win_ref[...])
    la_parts = _split3(la)

    r = lax.broadcasted_iota(jnp.int32, (ck, ck), 0)
    c = lax.broadcasted_iota(jnp.int32, (ck, ck), 1)
    causal = r >= c
    tril = causal.astype(BF16)

    bcs = [sum(_dot(tril, part[ci * ck:(ci + 1) * ck]) for part in la_parts) for ci in range(nck)]

    q_in, vb, att, dec = {}, {}, {}, {}
    for ci in range(nck):
        rows = slice(ci * ck, (ci + 1) * ck)
        bc = bcs[ci]
        for h in range(GLA_HEADS):
            bh = bc[:, h * GLA_DKH:(h + 1) * GLA_DKH]
            q = proj_ref[rows, h * GLA_DKH:(h + 1) * GLA_DKH] * (GLA_DKH ** -0.5)
            k = proj_ref[rows, GLA_DK + h * GLA_DKH:GLA_DK + (h + 1) * GLA_DKH]
            v = proj_ref[rows, 2 * GLA_DK + h * GLA_DVH:2 * GLA_DK + (h + 1) * GLA_DVH].astype(BF16)
            qi = (q * jnp.exp(bh)).astype(BF16)
            k_in = (k * jnp.exp(-bh)).astype(BF16)
            bl = bh[ck - 1:ck]
            kd = (k * jnp.exp(bl - bh)).astype(BF16)
            att[ci, h] = _dot_nt(qi, k_in)
            u_ref[ci, h] = _dot_tn(kd, v)
            q_in[ci, h], vb[ci, h], dec[ci, h] = qi, v, jnp.exp(bl)

    state = [s_ref[h] for h in range(GLA_HEADS)]
    for ci in range(nck):
        rows = slice(ci * ck, (ci + 1) * ck)
        for h in range(GLA_HEADS):
            am = jnp.where(causal, att[ci, h], 0.0).astype(BF16)
            o_ref[rows, h * GLA_DVH:(h + 1) * GLA_DVH] = (
                _dot(am, vb[ci, h]) + _dot(q_in[ci, h], state[h].astype(BF16)))
            dcol = jnp.transpose(jnp.broadcast_to(dec[ci, h], (GLA_DKH, GLA_DKH)))
            dfull = jnp.concatenate([dcol] * (GLA_DVH // GLA_DKH), axis=1)
            state[h] = dfull * state[h] + u_ref[ci, h]
    for h in range(GLA_HEADS):
        s_ref[h] = state[h]

    rt = tm // GLA_ROW_SPLITS
    for rs in range(GLA_ROW_SPLITS):
        rows = slice(rs * rt, (rs + 1) * rt)
        og = _gla_out_gate(o_ref[rows, :], proj_ref[rows, 2 * GLA_DK + GLA_DV:], ng_ref[...])
        z = ALPHA * x[rows] + _dot(og, wo_ref[...])
        y_ref[0, rows, :] = _layer_norm(z, g_ref[...], b_ref[...])

    @pl.when(ti == pl.num_programs(1) - 1)
    def _():
        so_ref[0] = s_ref[...]


def _gla_prompt(x, win, wa1, wa2, ba, ng, wo, g, b):
    bsz, seq, d = x.shape
    tm = min(GLA_TM, seq)
    nin = win.shape[1]
    st = (GLA_HEADS, GLA_DKH, GLA_DVH)
    return pl.pallas_call(
        _gla_prompt_kernel,
        grid=(bsz, seq // tm),
        in_specs=[
            pl.BlockSpec((1, tm, d), lambda bi, ti: (bi, ti, 0)),
            _resident(win.shape), _resident(wa1.shape), _resident(wa2.shape), _resident(ba.shape),
            _resident(ng.shape), _resident(wo.shape), _resident(g.shape), _resident(b.shape),
        ],
        out_specs=[pl.BlockSpec((1, tm, d), lambda bi, ti: (bi, ti, 0)),
                   pl.BlockSpec((1,) + st, lambda bi, ti: (bi, 0, 0, 0))],
        out_shape=[jax.ShapeDtypeStruct((bsz, seq, d), F32),
                   jax.ShapeDtypeStruct((bsz,) + st, F32)],
        scratch_shapes=[pltpu.VMEM(st, F32), pltpu.VMEM((tm, nin), F32),
                        pltpu.VMEM((tm // GLA_CHUNK,) + st, F32), pltpu.VMEM((tm, GLA_DV), F32)],
        compiler_params=_cparams(("arbitrary", "arbitrary")),
        name="gla_prompt",
    )(x, win, wa1, wa2, ba, ng, wo, g, b)


def _linear_kernel(x_ref, w_ref, b_ref, o_ref):
    o_ref[...] = _dot(x_ref[...].astype(BF16), w_ref[...]) + b_ref[...]


def _linear(x, w, bias):
    n, kdim = x.shape
    nout = w.shape[1]
    tn = next(t for t in (512, 256, 128) if nout % t == 0)
    return pl.pallas_call(
        _linear_kernel,
        grid=(nout // tn,),
        in_specs=[pl.BlockSpec((n, kdim), lambda j: (0, 0)),
                  pl.BlockSpec((kdim, tn), lambda j: (0, j)),
                  pl.BlockSpec((1, tn), lambda j: (0, j))],
        out_specs=pl.BlockSpec((n, tn), lambda j: (0, j)),
        out_shape=jax.ShapeDtypeStruct((n, nout), F32),
        compiler_params=_cparams(("arbitrary",)),
        name="sample_linear",
    )(x, w, bias)


def _linear_ln_kernel(h_ref, w_ref, r_ref, g_ref, b_ref, o_ref):
    z = ALPHA * r_ref[...] + _dot(h_ref[...].astype(BF16), w_ref[...])
    o_ref[...] = _layer_norm(z, g_ref[...], b_ref[...])


def _linear_ln(h, w, resid, g, b):
    n, d = resid.shape
    full = lambda a: pl.BlockSpec(a.shape, lambda i: (0,) * a.ndim)
    return pl.pallas_call(
        _linear_ln_kernel,
        grid=(1,),
        in_specs=[full(h), full(w), full(resid), full(g), full(b)],
        out_specs=pl.BlockSpec((n, d), lambda i: (0, 0)),
        out_shape=jax.ShapeDtypeStruct((n, d), F32),
        compiler_params=_cparams(("arbitrary",)),
        name="sample_linear_ln",
    )(h, w, resid, g, b)


def _attn_sample_kernel(q2_ref, kv_ref, ck_ref, cv_ref, cos_ref, sa_ref, sb_ref, sink_ref,
                        o2_ref, kn_ref, vn_ref):
    nseq, wbuf, _ = ck_ref.shape
    cos, sa, sb = cos_ref[...], sa_ref[...], sb_ref[...]
    q2 = (_rope_slab(q2_ref[...], cos, sa, sb) * ATTN_SCALE).astype(BF16)
    kv = kv_ref[...]
    k_all = _rope_slab(kv[:, :LANES], cos, sa, sb)
    v_all = kv[:, LANES:]
    last = lax.broadcasted_iota(jnp.int32, (wbuf, LANES), 0) == wbuf - 1
    sink = sink_ref[...]

    k_new, v_new = [], []
    for s in range(nseq):
        k_new.append(jnp.where(last, k_all[s:s + 1], pltpu.roll(ck_ref[s], wbuf - 1, 0)))
        v_new.append(jnp.where(last, v_all[s:s + 1], pltpu.roll(cv_ref[s], wbuf - 1, 0)))
        kn_ref[s] = k_new[s]
        vn_ref[s] = v_new[s]
    sc = [_dot_nt(q2[s * N_HEADS:(s + 1) * N_HEADS], k_new[s].astype(BF16)) for s in range(nseq)]
    ps = [_sink_softmax(sc[s], sink).astype(BF16) for s in range(nseq)]
    for s in range(nseq):
        o2_ref[s * N_HEADS:(s + 1) * N_HEADS, :] = _dot(ps[s], v_new[s].astype(BF16))


def _attn_sample(qkv, ck, cv, sinks):
    n = qkv.shape[0]
    wbuf = ck.shape[1]
    sb_ = SAMPLE_SEQ_BLOCK
    nq = N_HEADS * HEAD_DIM
    cos, sa, sb = _rope_tables(jnp.full((1,), PAST_LEN, dtype=F32))
    one = lambda a: pl.BlockSpec(a.shape, lambda i: (0,) * a.ndim)
    sink_col = sinks.reshape(N_HEADS, 1)
    q = qkv[:, :nq].reshape(n, N_HEADS, HEAD_DIM)
    zero = jnp.zeros_like(q)
    first_group = (jnp.arange(N_HEADS) < GROUP)[None, :, None]
    q2 = jnp.concatenate([jnp.where(first_group, q, zero), jnp.where(first_group, zero, q)], axis=-1)
    q2 = q2.reshape(n * N_HEADS, LANES)
    cache = pl.BlockSpec((sb_, wbuf, LANES), lambda i: (i, 0, 0))
    rows16 = pl.BlockSpec((sb_ * N_HEADS, LANES), lambda i: (i, 0))
    o2, k_s, v_s = pl.pallas_call(
        _attn_sample_kernel,
        grid=(n // sb_,),
        in_specs=[rows16, pl.BlockSpec((sb_, 2 * LANES), lambda i: (i, 0)), cache, cache,
                  one(cos), one(sa), one(sb), one(sink_col)],
        out_specs=[rows16, cache, cache],
        out_shape=[jax.ShapeDtypeStruct((n * N_HEADS, LANES), F32),
                   jax.ShapeDtypeStruct(ck.shape, F32), jax.ShapeDtypeStruct(cv.shape, F32)],
        compiler_params=_cparams(("arbitrary",)),
        name="attn_sample",
    )(q2, qkv[:, nq:], ck, cv, cos, sa, sb, sink_col)
    o2 = o2.reshape(n, N_HEADS, N_KV_HEADS, HEAD_DIM)
    o = jnp.where(first_group, o2[:, :, 0, :], o2[:, :, 1, :])
    return o.reshape(n, nq), k_s, v_s


def _gla_sample_kernel(proj_ref, z_ref, s0_ref, ng_ref, o_ref, sn_ref, oacc_ref):
    proj = proj_ref[...]
    la = jax.nn.log_sigmoid(z_ref[...]) * (1.0 / GLA_TAU)
    nseq = proj.shape[0]
    pad = jnp.zeros((LANES - nseq, GLA_DKH), F32)

    def cols(t):
        return jnp.transpose(jnp.concatenate([t, pad], axis=0))

    for h in range(GLA_HEADS):
        b = la[:, h * GLA_DKH:(h + 1) * GLA_DKH]
        q = proj[:, h * GLA_DKH:(h + 1) * GLA_DKH] * (GLA_DKH ** -0.5)
        k = proj[:, GLA_DK + h * GLA_DKH:GLA_DK + (h + 1) * GLA_DKH]
        v = proj[:, 2 * GLA_DK + h * GLA_DVH:2 * GLA_DK + (h + 1) * GLA_DVH]
        eb = jnp.exp(b)
        q_in = q * eb
        k_in = k * jnp.exp(-b)
        att = jnp.sum(q_in * k_in, axis=-1, keepdims=True)
        kd = k * jnp.exp(b - b)
        qc, kc, ec = cols(q_in), cols(kd), cols(eb)
        for s in range(nseq):
            s_old = s0_ref[s, h]
            vrow = v[s:s + 1]
            o_row = att[s:s + 1] * vrow + jnp.sum(qc[:, s:s + 1] * s_old, axis=0, keepdims=True)
            oacc_ref[s:s + 1, h * GLA_DVH:(h + 1) * GLA_DVH] = o_row
            sn_ref[s, h] = ec[:, s:s + 1] * s_old + kc[:, s:s + 1] * vrow
    og = _gla_out_gate(oacc_ref[...], proj[:, 2 * GLA_DK + GLA_DV:], ng_ref[...])
    o_ref[...] = og.astype(F32)


def _gla_sample(proj, z, s0, ng):
    n = proj.shape[0]
    sb_ = SAMPLE_SEQ_BLOCK
    st = pl.BlockSpec((sb_,) + s0.shape[1:], lambda i: (i, 0, 0, 0))
    return pl.pallas_call(
        _gla_sample_kernel,
        grid=(n // sb_,),
        in_specs=[pl.BlockSpec((sb_, proj.shape[1]), lambda i: (i, 0)),
                  pl.BlockSpec((sb_, z.shape[1]), lambda i: (i, 0)), st,
                  pl.BlockSpec(ng.shape, lambda i: (0, 0))],
        out_specs=[pl.BlockSpec((sb_, GLA_DV), lambda i: (i, 0)), st],
        out_shape=[jax.ShapeDtypeStruct((n, GLA_DV), F32), jax.ShapeDtypeStruct(s0.shape, F32)],
        scratch_shapes=[pltpu.VMEM((sb_, GLA_DV), F32)],
        compiler_params=_cparams(("arbitrary",)),
        name="gla_sample",
    )(proj, z, s0, ng)


def _row(v):
    return v.reshape(1, -1)


def kernel(x_prompt, x_sample, cache_k, cache_v, state_gla, state_conv, attn_w_qkv, attn_b_qkv, attn_sinks, attn_w_o, gla_w_in, gla_w_a1, gla_w_a2, gla_b_a, gla_norm_g, gla_w_o, ffn_w_up, ffn_conv_w, ffn_conv_b, ffn_w_down, ln_mix_g, ln_mix_b, ln_ffn_g, ln_ffn_b):
    bsz, seq, d = x_prompt.shape
    nsmp = x_sample.shape[0]
    xp = x_prompt
    xs = x_sample.reshape(nsmp, d)
    zeros = lambda n: jnp.zeros((1, n), F32)

    kp_l, vp_l, ks_l, vs_l, sp_l, ss_l, cp_l, cs_l = [], [], [], [], [], [], [], []
    for i in range(DEPTH):
        j = i // 2
        g_mix, b_mix = _row(ln_mix_g[i]), _row(ln_mix_b[i])
        if i % 2 == 0:
            wqkv = attn_w_qkv[j].astype(BF16)
            bqkv = _row(attn_b_qkv[j])
            wo = attn_w_o[j].astype(BF16)
            xp, kp, vp = _attn_prompt(xp, wqkv, bqkv, attn_sinks[j], wo, g_mix, b_mix)
            kv_shape = (bsz, WINDOW, N_KV_HEADS, HEAD_DIM)
            kp_l.append(kp.reshape(kv_shape))
            vp_l.append(vp.reshape(kv_shape))

            wbuf = cache_k.shape[2]
            qkv_s = _linear(xs, wqkv, bqkv)
            o_s, k_s, v_s = _attn_sample(qkv_s, cache_k[j].reshape(nsmp, wbuf, LANES),
                                         cache_v[j].reshape(nsmp, wbuf, LANES), attn_sinks[j])
            xs = _linear_ln(o_s, wo, xs, g_mix, b_mix)
            ks_l.append(k_s.reshape(cache_k.shape[1:]))
            vs_l.append(v_s.reshape(cache_v.shape[1:]))
        else:
            win = gla_w_in[j].astype(BF16)
            wa1 = jnp.pad(gla_w_a1[j], ((0, 0), (0, LANES - GLA_RANK))).astype(BF16)
            wa2 = jnp.pad(gla_w_a2[j], ((0, LANES - GLA_RANK), (0, 0))).astype(BF16)
            ba = _row(gla_b_a[j])
            ng = _row(gla_norm_g[j])
            wo = gla_w_o[j].astype(BF16)
            xp, sp = _gla_prompt(xp, win, wa1, wa2, ba, ng, wo, g_mix, b_mix)
            sp_l.append(sp)

            proj_s = _linear(xs, win, zeros(win.shape[1]))
            t1 = _linear(xs, wa1, zeros(LANES))
            z_s = _linear(t1, wa2, ba)
            o_s, s_s = _gla_sample(proj_s, z_s, state_gla[j], ng)
            xs = _linear_ln(o_s, wo, xs, g_mix, b_mix)
            ss_l.append(s_s)

        wup, wd = ffn_w_up[i].astype(BF16), ffn_w_down[i].astype(BF16)
        cw, cb = ffn_conv_w[i], _row(ffn_conv_b[i])
        g_ffn, b_ffn = _row(ln_ffn_g[i]), _row(ln_ffn_b[i])
        xp2, cp = _ffn_prompt(xp.reshape(bsz * seq, d), seq, wup, wd, cw, cb, g_ffn, b_ffn)
        xp = xp2.reshape(bsz, seq, d)
        cp_l.append(cp)
        xs, a_s = _ffn_sample(xs, state_conv[i, :, 0], state_conv[i, :, 1], wup, wd, cw, cb,
                              g_ffn, b_ffn)
        cs_l.append(jnp.stack([state_conv[i, :, 1], a_s], axis=1))

    return (xp, xs.reshape(x_sample.shape), jnp.stack(kp_l), jnp.stack(vp_l), jnp.stack(ks_l),
            jnp.stack(vs_l), jnp.stack(sp_l), jnp.stack(ss_l), jnp.stack(cp_l), jnp.stack(cs_l))
```

```python
import functools

import jax
import jax.numpy as jnp
from jax import lax
from jax.experimental import pallas as pl
from jax.experimental.pallas import tpu as pltpu

F32 = jnp.float32
BF16 = jnp.bfloat16

D_MODEL = 1024
DEPTH = 2
PAST_LEN = 16384
N_HEADS = 16
N_KV_HEADS = 2
HEAD_DIM = 64
GROUP = N_HEADS // N_KV_HEADS
WINDOW = 128
ROPE_THETA = 10000.0
ATTN_SCALE = HEAD_DIM ** -0.5
GLA_HEADS = 4
GLA_DK = D_MODEL // 2
GLA_DV = D_MODEL
GLA_DKH = GLA_DK // GLA_HEADS
GLA_DVH = GLA_DV // GLA_HEADS
GLA_RANK = 16
GLA_TAU = 16.0
GLA_CHUNK = 64
D_FF = 2816
CONV_W = 3
ALPHA = (2 * DEPTH) ** 0.25
LN_EPS = 1e-5
NORM_EPS = 1e-6
NEG_INF = -1e30

LANES = 128
SUBLANES = 8
VMEM_LIMIT_BYTES = 56 * 1024 * 1024

ATTN_TQ = 1024
FFN_TM = 1024
FFN_TF = 256
FFN_ROW_SPLITS = 4
GLA_TM = 512
GLA_ROW_SPLITS = 4
SAMPLE_SEQ_BLOCK = 8


def _cparams(sem):
    return pltpu.CompilerParams(dimension_semantics=sem, vmem_limit_bytes=VMEM_LIMIT_BYTES)


def _resident(shape):
    nd = len(shape)
    return pl.BlockSpec(shape, lambda *_: (0,) * nd, pipeline_mode=pl.Buffered(1))


def _resident_layer(stacked, layer):
    nd = stacked.ndim - 1
    return pl.BlockSpec((None,) + stacked.shape[1:], lambda *_: (layer,) + (0,) * nd,
                        pipeline_mode=pl.Buffered(1))


def _layer_norm(z, g, b):
    mu = jnp.mean(z, axis=-1, keepdims=True)
    zc = z - mu
    var = jnp.mean(zc * zc, axis=-1, keepdims=True)
    return zc * lax.rsqrt(var + LN_EPS) * g + b


def _gelu(c):
    return 0.5 * c * (1.0 + lax.erf(c * (2.0 ** -0.5)))


def _dot(a, b):
    return jnp.dot(a, b, preferred_element_type=F32)


def _dot_nt(a, b):
    return lax.dot_general(a, b, (((1,), (1,)), ((), ())), preferred_element_type=F32)


def _dot_tn(a, b):
    return lax.dot_general(a, b, (((0,), (0,)), ((), ())), preferred_element_type=F32)


def _rope_tables(pos):
    inv = 1.0 / (ROPE_THETA ** (jnp.arange(0, HEAD_DIM, 2, dtype=F32) / HEAD_DIM))
    ang = pos[:, None] * inv[None, :]
    cos, sin = jnp.cos(ang), jnp.sin(ang)
    zero = jnp.zeros_like(sin)
    cos_h = jnp.concatenate([cos, cos], -1)
    sa_h = jnp.concatenate([-sin, zero], -1)
    sb_h = jnp.concatenate([zero, sin], -1)
    two = lambda t: jnp.concatenate([t, t], -1)
    return two(cos_h), two(sa_h), two(sb_h)


def _rope_slab(s, cos, sa, sb):
    return (s * cos + pltpu.roll(s, LANES - HEAD_DIM // 2, 1) * sa
            + pltpu.roll(s, HEAD_DIM // 2, 1) * sb)


def _sink_softmax(s, sink):
    m = jnp.maximum(jnp.max(s, axis=-1, keepdims=True), sink)
    p = jnp.exp(s - m)
    den = jnp.sum(p, axis=-1, keepdims=True) + jnp.exp(sink - m)
    return p * (1.0 / den)


def _lane_split_pair(t):
    lo = lax.broadcasted_iota(jnp.int32, t.shape, 1) < HEAD_DIM
    tr = pltpu.roll(t, HEAD_DIM, 1)
    z = jnp.zeros_like(t)
    return (jnp.where(lo, t, z), jnp.where(lo, z, tr), jnp.where(lo, tr, z), jnp.where(lo, z, t))


def _attn_prompt_kernel(sinks_ref, x_ref, cos_ref, sa_ref, sb_ref, wqkv_ref, bqkv_ref, wo_ref,
                        g_ref, b_ref, y_ref, kn_ref, vn_ref, kprev_ref, vprev_ref):
    ti = pl.program_id(1)
    nq = N_HEADS * HEAD_DIM
    w = WINDOW

    @pl.when(ti == 0)
    def _():
        kprev_ref[...] = jnp.zeros_like(kprev_ref)
        vprev_ref[...] = jnp.zeros_like(vprev_ref)

    nblk = x_ref.shape[1] // w
    nslab = nq // LANES
    spg = nslab // N_KV_HEADS

    t = lax.broadcasted_iota(jnp.int32, (w, 2 * w), 0)
    s = lax.broadcasted_iota(jnp.int32, (w, 2 * w), 1)
    band = (s > t) & (s <= t + w)
    band_first = band & ((ti > 0) | (s >= w))

    def project(blk):
        rows = slice(blk * w, (blk + 1) * w)
        x = x_ref[0, rows, :]
        qkv = _dot(x.astype(BF16), wqkv_ref[...]) + bqkv_ref[...]
        cos, sa, sb = cos_ref[rows, :], sa_ref[rows, :], sb_ref[rows, :]
        k = _rope_slab(qkv[:, nq:nq + LANES], cos, sa, sb)
        v = qkv[:, nq + LANES:nq + 2 * LANES]
        q = [(_rope_slab(qkv[:, j * LANES:(j + 1) * LANES], cos, sa, sb) * ATTN_SCALE).astype(BF16)
             for j in range(nslab)]
        return dict(x=x, q=q, k=k, v=v, ks=_lane_split_pair(k.astype(BF16)),
                    vs=_lane_split_pair(v.astype(BF16)))

    def finish(blk, st, pv):
        slabs = []
        for g in range(N_KV_HEADS):
            og = pv[g, 0] + pv[g, 1]
            slabs += [og[i * w:(i + 1) * w] for i in range(spg)]
        o = jnp.concatenate(slabs, axis=1).astype(BF16)
        z = ALPHA * st["x"] + _dot(o, wo_ref[...])
        y_ref[0, blk * w:(blk + 1) * w, :] = _layer_norm(z, g_ref[...], b_ref[...])

    units = [(g, par) for g in range(N_KV_HEADS) for par in range(2)]
    prev = dict(ks=_lane_split_pair(kprev_ref[...].astype(BF16)),
                vs=_lane_split_pair(vprev_ref[...].astype(BF16)))
    stages = {-1: prev, 0: project(0)}
    pvs = {}
    for blk in range(nblk):
        if blk + 1 < nblk:
            stages[blk + 1] = project(blk + 1)
        st, pst = stages[blk], stages[blk - 1]
        mask = band_first if blk == 0 else band

        def scores(u):
            g, par = u
            qg = jnp.concatenate([st["q"][g * spg + i] for i in range(spg)], axis=0)
            kmat = jnp.concatenate([pst["ks"][2 * g + par], st["ks"][2 * g + par]], axis=0)
            return _dot_nt(qg, kmat)

        pv = {}
        sc_nxt = scores(units[0])
        for n, (g, par) in enumerate(units):
            sc = sc_nxt
            if n + 1 < len(units):
                sc_nxt = scores(units[n + 1])
            ps = []
            for i in range(spg):
                head = 2 * (g * spg + i) + par
                si = jnp.where(mask, sc[i * w:(i + 1) * w], NEG_INF)
                ps.append(_sink_softmax(si, sinks_ref[head]).astype(BF16))
            vmat = jnp.concatenate([pst["vs"][2 * g + par], st["vs"][2 * g + par]], axis=0)
            pv[g, par] = _dot(jnp.concatenate(ps, axis=0), vmat)
        pvs[blk] = pv
        if blk >= 1:
            finish(blk - 1, stages[blk - 1], pvs[blk - 1])
    finish(nblk - 1, stages[nblk - 1], pvs[nblk - 1])

    k_last, v_last = stages[nblk - 1]["k"], stages[nblk - 1]["v"]
    kprev_ref[...] = k_last
    vprev_ref[...] = v_last

    @pl.when(ti == pl.num_programs(1) - 1)
    def _():
        kn_ref[0] = k_last
        vn_ref[0] = v_last


def _attn_prompt(x, wqkv, bqkv, sinks, wo, g, b):
    bsz, seq, d = x.shape
    tq = min(ATTN_TQ, seq)
    nqkv = wqkv.shape[1]
    cos, sa, sb = _rope_tables(jnp.arange(seq, dtype=F32))
    tab = pl.BlockSpec((tq, LANES), lambda bi, ti: (ti, 0))
    kv_out = pl.BlockSpec((1, WINDOW, LANES), lambda bi, ti: (bi, 0, 0))
    return pl.pallas_call(
        _attn_prompt_kernel,
        grid=(bsz, seq // tq),
        in_specs=[
            pl.BlockSpec(memory_space=pltpu.SMEM),
            pl.BlockSpec((1, tq, d), lambda bi, ti: (bi, ti, 0)),
            tab, tab, tab,
            _resident((d, nqkv)), _resident((1, nqkv)), _resident((d, d)),
            _resident((1, d)), _resident((1, d)),
        ],
        out_specs=[pl.BlockSpec((1, tq, d), lambda bi, ti: (bi, ti, 0)), kv_out, kv_out],
        out_shape=[jax.ShapeDtypeStruct((bsz, seq, d), F32),
                   jax.ShapeDtypeStruct((bsz, WINDOW, LANES), F32),
                   jax.ShapeDtypeStruct((bsz, WINDOW, LANES), F32)],
        scratch_shapes=[pltpu.VMEM((WINDOW, LANES), F32), pltpu.VMEM((WINDOW, LANES), F32)],
        compiler_params=_cparams(("arbitrary", "arbitrary")),
        name="attn_prompt",
    )(sinks, x, cos, sa, sb, wqkv, bqkv, wo, g, b)


def _conv_gate(a, a1, a2, u, cw, cb):
    c = cb + cw[0:1] * a2 + cw[1:2] * a1 + cw[2:3] * a
    return (_gelu(c) * u).astype(BF16)


def _ffn_prompt_kernel(x_ref, wup_ref, wd_ref, cw_ref, cb_ref, g_ref, b_ref,
                       y_ref, cs_ref, carry_ref, h_ref, *, tiles_per_seq):
    i = pl.program_id(0)
    pos = i % tiles_per_seq

    @pl.when(pos == 0)
    def _():
        carry_ref[...] = jnp.zeros_like(carry_ref)

    x = x_ref[...]
    xb = x.astype(BF16)
    tm = x.shape[0]
    dff = wd_ref.shape[0]
    tf = FFN_TF
    row = lax.broadcasted_iota(jnp.int32, (SUBLANES, tf), 0)
    for j in range(dff // tf):
        sl = slice(j * tf, (j + 1) * tf)
        a = _dot(xb, wup_ref[:, sl])
        u = _dot(xb, wup_ref[:, dff + j * tf:dff + (j + 1) * tf])
        prev = carry_ref[:, sl]
        p1 = prev[SUBLANES - 1:SUBLANES]
        p2 = prev[SUBLANES - 2:SUBLANES - 1]
        a1 = pltpu.roll(a, 1, 0)
        a2 = pltpu.roll(a, 2, 0)
        a1_top = jnp.where(row == 0, p1, a1[:SUBLANES])
        a2_top = jnp.where(row == 0, p2, jnp.where(row == 1, p1, a2[:SUBLANES]))
        a1 = jnp.concatenate([a1_top, a1[SUBLANES:]], axis=0)
        a2 = jnp.concatenate([a2_top, a2[SUBLANES:]], axis=0)
        carry_ref[:, sl] = a[tm - SUBLANES:tm]
        h_ref[:, sl] = _conv_gate(a, a1, a2, u, cw_ref[:, sl], cb_ref[:, sl])

    rt = tm // FFN_ROW_SPLITS
    for r in range(FFN_ROW_SPLITS):
        rows = slice(r * rt, (r + 1) * rt)
        z = ALPHA * x[rows] + _dot(h_ref[rows, :], wd_ref[...])
        y_ref[rows, :] = _layer_norm(z, g_ref[...], b_ref[...])

    @pl.when(pos == tiles_per_seq - 1)
    def _():
        cs_ref[0] = carry_ref[SUBLANES - (CONV_W - 1):SUBLANES, :]


def _ffn_prompt(x, seq, layer, wup, wd, cw, cb, g, b):
    n, d = x.shape
    tm = min(FFN_TM, seq)
    tiles_per_seq = seq // tm
    dff = wd.shape[1]
    return pl.pallas_call(
        functools.partial(_ffn_prompt_kernel, tiles_per_seq=tiles_per_seq),
        grid=(n // tm,),
        in_specs=[pl.BlockSpec((tm, d), lambda i: (i, 0))]
        + [_resident_layer(a, layer) for a in (wup, wd, cw, cb, g, b)],
        out_specs=[pl.BlockSpec((tm, d), lambda i: (i, 0)),
                   pl.BlockSpec((1, CONV_W - 1, dff), lambda i: (i // tiles_per_seq, 0, 0))],
        out_shape=[jax.ShapeDtypeStruct((n, d), F32),
                   jax.ShapeDtypeStruct((n // seq, CONV_W - 1, dff), F32)],
        scratch_shapes=[pltpu.VMEM((SUBLANES, dff), F32), pltpu.VMEM((tm, dff), BF16)],
        compiler_params=_cparams(("arbitrary",)),
        name="ffn_prompt",
    )(x, wup, wd, cw, cb, g, b)


def _ffn_sample_kernel(x_ref, s0_ref, s1_ref, wa_ref, wu_ref, wd_ref, cw_ref, cb_ref, g_ref, b_ref,
                       y_ref, a_ref, acc_ref):
    j = pl.program_id(0)
    x = x_ref[...]
    xb = x.astype(BF16)
    a = _dot(xb, wa_ref[...])
    u = _dot(xb, wu_ref[...])
    a_ref[...] = a
    h = _conv_gate(a, s1_ref[...], s0_ref[...], u, cw_ref[...], cb_ref[...])
    dn = _dot(h, wd_ref[...])

    @pl.when(j == 0)
    def _():
        acc_ref[...] = dn

    @pl.when(j > 0)
    def _():
        acc_ref[...] += dn

    @pl.when(j == pl.num_programs(0) - 1)
    def _():
        y_ref[...] = _layer_norm(ALPHA * x + acc_ref[...], g_ref[...], b_ref[...])


def _ffn_sample(x, s0, s1, layer, wup, wd, cw, cb, g, b):
    n, d = x.shape
    dff = wd.shape[1]
    tf = FFN_TF
    nchunk = dff // tf
    col = pl.BlockSpec((n, tf), lambda j: (0, j))
    return pl.pallas_call(
        _ffn_sample_kernel,
        grid=(nchunk,),
        in_specs=[
            pl.BlockSpec((n, d), lambda j: (0, 0)), col, col,
            pl.BlockSpec((None, d, tf), lambda j: (layer, 0, j)),
            pl.BlockSpec((None, d, tf), lambda j: (layer, 0, j + nchunk)),
            pl.BlockSpec((None, tf, d), lambda j: (layer, j, 0)),
            pl.BlockSpec((None, CONV_W, tf), lambda j: (layer, 0, j)),
            pl.BlockSpec((None, 1, tf), lambda j: (layer, 0, j)),
            pl.BlockSpec((None, 1, d), lambda j: (layer, 0, 0)),
            pl.BlockSpec((None, 1, d), lambda j: (layer, 0, 0)),
        ],
        out_specs=[pl.BlockSpec((n, d), lambda j: (0, 0)), col],
        out_shape=[jax.ShapeDtypeStruct((n, d), F32), jax.ShapeDtypeStruct((n, dff), F32)],
        scratch_shapes=[pltpu.VMEM((n, d), F32)],
        compiler_params=_cparams(("arbitrary",)),
        name="ffn_sample",
    )(x, s0, s1, wup, wup, wd, cw, cb, g, b)


def _gla_decay_logits(xb, wa1_ref, wa2_ref, ba_ref):
    t1 = _dot(xb, wa1_ref[...])
    z = _dot(t1.astype(BF16), wa2_ref[...]) + ba_ref[...]
    return jax.nn.log_sigmoid(z) * (1.0 / GLA_TAU)


def _gla_out_gate(o, gate, ng):
    outs = []
    for h in range(GLA_HEADS):
        oh = o[:, h * GLA_DVH:(h + 1) * GLA_DVH]
        ms = jnp.mean(oh * oh, axis=-1, keepdims=True)
        outs.append(oh * lax.rsqrt(ms + NORM_EPS) * ng)
    on = jnp.concatenate(outs, axis=1)
    return (on * jax.nn.silu(gate)).astype(BF16)


def _split3(t):
    hi = t.astype(BF16)
    r1 = t - hi.astype(F32)
    mid = r1.astype(BF16)
    lo = (r1 - mid.astype(F32)).astype(BF16)
    return hi, mid, lo


def _gla_prompt_kernel(x_ref, win_ref, wa1_ref, wa2_ref, ba_ref, ng_ref, wo_ref, g_ref, b_ref,
                       y_ref, so_ref, s_ref, proj_ref, u_ref, o_ref):
    ti = pl.program_id(1)
    ck = GLA_CHUNK

    @pl.when(ti == 0)
    def _():
        s_ref[...] = jnp.zeros_like(s_ref)

    x = x_ref[0]
    xb = x.astype(BF16)
    tm = x.shape[0]
    nck = tm // ck
    la = _gla_decay_logits(xb, wa1_ref, wa2_ref, ba_ref)
    c_v, c_g = 2 * GLA_DK, 2 * GLA_DK + GLA_DV
    proj_ref[:, :c_v] = _dot(xb, win_ref[:, :c_v])
    la_parts = _split3(la)

    r = lax.broadcasted_iota(jnp.int32, (ck, ck), 0)
    c = lax.broadcasted_iota(jnp.int32, (ck, ck), 1)
    causal = r >= c
    tril = causal.astype(BF16)

    tril3 = jnp.concatenate([tril] * 3 + [jnp.zeros_like(tril)], axis=1)
    zrows = jnp.zeros((ck, GLA_DK), BF16)
    bcs = [_dot(tril3, jnp.concatenate([part[ci * ck:(ci + 1) * ck] for part in la_parts] + [zrows],
                                       axis=0))
           for ci in range(nck)]

    proj_ref[:, c_v:c_g] = _dot(xb, win_ref[:, c_v:c_g])
    q_in, vb, att, dec = {}, {}, {}, {}
    for ci in range(nck):
        rows = slice(ci * ck, (ci + 1) * ck)
        bc = bcs[ci]
        for h in range(GLA_HEADS):
            bh = bc[:, h * GLA_DKH:(h + 1) * GLA_DKH]
            q = proj_ref[rows, h * GLA_DKH:(h + 1) * GLA_DKH] * (GLA_DKH ** -0.5)
            k = proj_ref[rows, GLA_DK + h * GLA_DKH:GLA_DK + (h + 1) * GLA_DKH]
            v = proj_ref[rows, 2 * GLA_DK + h * GLA_DVH:2 * GLA_DK + (h + 1) * GLA_DVH].astype(BF16)
            qi = (q * jnp.exp(bh)).astype(BF16)
            k_in = (k * jnp.exp(-bh)).astype(BF16)
            bl = bh[ck - 1:ck]
            kd = (k * jnp.exp(bl - bh)).astype(BF16)
            att[ci, h] = _dot_nt(qi, k_in)
            u_ref[ci, h] = _dot_tn(kd, v)
            q_in[ci, h], vb[ci, h], dec[ci, h] = qi, v, jnp.exp(bl)

    proj_ref[:, c_g:] = _dot(xb, win_ref[:, c_g:])
    state = [s_ref[h] for h in range(GLA_HEADS)]
    zpad_l = jnp.zeros((ck, ck), BF16)
    zpad_r = jnp.zeros((ck, GLA_DVH), BF16)
    for ci in range(nck):
        rows = slice(ci * ck, (ci + 1) * ck)
        for h in range(GLA_HEADS):
            am = jnp.where(causal, att[ci, h], 0.0).astype(BF16)
            lhs = jnp.concatenate([q_in[ci, h], am, zpad_l], axis=1)
            rhs = jnp.concatenate([state[h].astype(BF16), vb[ci, h], zpad_r], axis=0)
            o_ref[rows, h * GLA_DVH:(h + 1) * GLA_DVH] = _dot(lhs, rhs)
            dcol = jnp.transpose(jnp.broadcast_to(dec[ci, h], (GLA_DKH, GLA_DKH)))
            dfull = jnp.concatenate([dcol] * (GLA_DVH // GLA_DKH), axis=1)
            state[h] = dfull * state[h] + u_ref[ci, h]
    for h in range(GLA_HEADS):
        s_ref[h] = state[h]

    rt = tm // GLA_ROW_SPLITS
    for rs in range(GLA_ROW_SPLITS):
        rows = slice(rs * rt, (rs + 1) * rt)
        og = _gla_out_gate(o_ref[rows, :], proj_ref[rows, 2 * GLA_DK + GLA_DV:], ng_ref[...])
        z = ALPHA * x[rows] + _dot(og, wo_ref[...])
        y_ref[0, rows, :] = _layer_norm(z, g_ref[...], b_ref[...])

    @pl.when(ti == pl.num_programs(1) - 1)
    def _():
        so_ref[0] = s_ref[...]


def _gla_prompt(x, win, wa1, wa2, ba, ng, wo, g, b):
    bsz, seq, d = x.shape
    tm = min(GLA_TM, seq)
    nin = win.shape[1]
    st = (GLA_HEADS, GLA_DKH, GLA_DVH)
    return pl.pallas_call(
        _gla_prompt_kernel,
        grid=(bsz, seq // tm),
        in_specs=[
            pl.BlockSpec((1, tm, d), lambda bi, ti: (bi, ti, 0)),
            _resident(win.shape), _resident(wa1.shape), _resident(wa2.shape), _resident(ba.shape),
            _resident(ng.shape), _resident(wo.shape), _resident(g.shape), _resident(b.shape),
        ],
        out_specs=[pl.BlockSpec((1, tm, d), lambda bi, ti: (bi, ti, 0)),
                   pl.BlockSpec((1,) + st, lambda bi, ti: (bi, 0, 0, 0))],
        out_shape=[jax.ShapeDtypeStruct((bsz, seq, d), F32),
                   jax.ShapeDtypeStruct((bsz,) + st, F32)],
        scratch_shapes=[pltpu.VMEM(st, F32), pltpu.VMEM((tm, nin), F32),
                        pltpu.VMEM((tm // GLA_CHUNK,) + st, F32), pltpu.VMEM((tm, GLA_DV), F32)],
        compiler_params=_cparams(("arbitrary", "arbitrary")),
        name="gla_prompt",
    )(x, win, wa1, wa2, ba, ng, wo, g, b)


def _linear_kernel(x_ref, w_ref, b_ref, o_ref):
    o_ref[...] = _dot(x_ref[...].astype(BF16), w_ref[...]) + b_ref[...]


def _linear(x, w, bias):
    n, kdim = x.shape
    nout = w.shape[1]
    tn = next(t for t in (512, 256, 128) if nout % t == 0)
    return pl.pallas_call(
        _linear_kernel,
        grid=(nout // tn,),
        in_specs=[pl.BlockSpec((n, kdim), lambda j: (0, 0)),
                  pl.BlockSpec((kdim, tn), lambda j: (0, j)),
                  pl.BlockSpec((1, tn), lambda j: (0, j))],
        out_specs=pl.BlockSpec((n, tn), lambda j: (0, j)),
        out_shape=jax.ShapeDtypeStruct((n, nout), F32),
        compiler_params=_cparams(("arbitrary",)),
        name="sample_linear",
    )(x, w, bias)


def _linear_ln_kernel(h_ref, w_ref, r_ref, g_ref, b_ref, o_ref):
    z = ALPHA * r_ref[...] + _dot(h_ref[...].astype(BF16), w_ref[...])
    o_ref[...] = _layer_norm(z, g_ref[...], b_ref[...])


def _linear_ln(h, w, resid, g, b):
    n, d = resid.shape
    full = lambda a: pl.BlockSpec(a.shape, lambda i: (0,) * a.ndim)
    return pl.pallas_call(
        _linear_ln_kernel,
        grid=(1,),
        in_specs=[full(h), full(w), full(resid), full(g), full(b)],
        out_specs=pl.BlockSpec((n, d), lambda i: (0, 0)),
        out_shape=jax.ShapeDtypeStruct((n, d), F32),
        compiler_params=_cparams(("arbitrary",)),
        name="sample_linear_ln",
    )(h, w, resid, g, b)


def _attn_sample_kernel(q2_ref, kv_ref, ck_ref, cv_ref, cos_ref, sa_ref, sb_ref, sink_ref,
                        o2_ref, kn_ref, vn_ref):
    nseq, wbuf, _ = ck_ref.shape
    cos, sa, sb = cos_ref[...], sa_ref[...], sb_ref[...]
    q2 = (_rope_slab(q2_ref[...], cos, sa, sb) * ATTN_SCALE).astype(BF16)
    kv = kv_ref[...]
    k_all = _rope_slab(kv[:, :LANES], cos, sa, sb)
    v_all = kv[:, LANES:]
    last = lax.broadcasted_iota(jnp.int32, (wbuf, LANES), 0) == wbuf - 1
    sink = sink_ref[...]

    k_new, v_new = [], []
    for s in range(nseq):
        k_new.append(jnp.where(last, k_all[s:s + 1], pltpu.roll(ck_ref[s], wbuf - 1, 0)))
        v_new.append(jnp.where(last, v_all[s:s + 1], pltpu.roll(cv_ref[s], wbuf - 1, 0)))
        kn_ref[s] = k_new[s]
        vn_ref[s] = v_new[s]
    sc = [_dot_nt(q2[s * N_HEADS:(s + 1) * N_HEADS], k_new[s].astype(BF16)) for s in range(nseq)]
    ps = [_sink_softmax(sc[s], sink).astype(BF16) for s in range(nseq)]
    for s in range(nseq):
        o2_ref[s * N_HEADS:(s + 1) * N_HEADS, :] = _dot(ps[s], v_new[s].astype(BF16))


def _attn_sample(qkv, ck, cv, sinks):
    n = qkv.shape[0]
    wbuf = ck.shape[1]
    sb_ = SAMPLE_SEQ_BLOCK
    nq = N_HEADS * HEAD_DIM
    cos, sa, sb = _rope_tables(jnp.full((1,), PAST_LEN, dtype=F32))
    one = lambda a: pl.BlockSpec(a.shape, lambda i: (0,) * a.ndim)
    sink_col = sinks.reshape(N_HEADS, 1)
    q = qkv[:, :nq].reshape(n, N_HEADS, HEAD_DIM)
    zero = jnp.zeros_like(q)
    first_group = (jnp.arange(N_HEADS) < GROUP)[None, :, None]
    q2 = jnp.concatenate([jnp.where(first_group, q, zero), jnp.where(first_group, zero, q)], axis=-1)
    q2 = q2.reshape(n * N_HEADS, LANES)
    cache = pl.BlockSpec((sb_, wbuf, LANES), lambda i: (i, 0, 0))
    rows16 = pl.BlockSpec((sb_ * N_HEADS, LANES), lambda i: (i, 0))
    o2, k_s, v_s = pl.pallas_call(
        _attn_sample_kernel,
        grid=(n // sb_,),
        in_specs=[rows16, pl.BlockSpec((sb_, 2 * LANES), lambda i: (i, 0)), cache, cache,
                  one(cos), one(sa), one(sb), one(sink_col)],
        out_specs=[rows16, cache, cache],
        out_shape=[jax.ShapeDtypeStruct((n * N_HEADS, LANES), F32),
                   jax.ShapeDtypeStruct(ck.shape, F32), jax.ShapeDtypeStruct(cv.shape, F32)],
        compiler_params=_cparams(("arbitrary",)),
        name="attn_sample",
    )(q2, qkv[:, nq:], ck, cv, cos, sa, sb, sink_col)
    o2 = o2.reshape(n, N_HEADS, N_KV_HEADS, HEAD_DIM)
    o = jnp.where(first_group, o2[:, :, 0, :], o2[:, :, 1, :])
    return o.reshape(n, nq), k_s, v_s


def _gla_sample_kernel(proj_ref, z_ref, s0_ref, ng_ref, o_ref, sn_ref, oacc_ref):
    proj = proj_ref[...]
    la = jax.nn.log_sigmoid(z_ref[...]) * (1.0 / GLA_TAU)
    nseq = proj.shape[0]
    pad = jnp.zeros((LANES - nseq, GLA_DKH), F32)

    def cols(t):
        return jnp.transpose(jnp.concatenate([t, pad], axis=0))

    for h in range(GLA_HEADS):
        b = la[:, h * GLA_DKH:(h + 1) * GLA_DKH]
        q = proj[:, h * GLA_DKH:(h + 1) * GLA_DKH] * (GLA_DKH ** -0.5)
        k = proj[:, GLA_DK + h * GLA_DKH:GLA_DK + (h + 1) * GLA_DKH]
        v = proj[:, 2 * GLA_DK + h * GLA_DVH:2 * GLA_DK + (h + 1) * GLA_DVH]
        eb = jnp.exp(b)
        q_in = q * eb
        k_in = k * jnp.exp(-b)
        att = jnp.sum(q_in * k_in, axis=-1, keepdims=True)
        kd = k * jnp.exp(b - b)
        qc, kc, ec = cols(q_in), cols(kd), cols(eb)
        for s in range(nseq):
            s_old = s0_ref[s, h]
            vrow = v[s:s + 1]
            o_row = att[s:s + 1] * vrow + jnp.sum(qc[:, s:s + 1] * s_old, axis=0, keepdims=True)
            oacc_ref[s:s + 1, h * GLA_DVH:(h + 1) * GLA_DVH] = o_row
            sn_ref[s, h] = ec[:, s:s + 1] * s_old + kc[:, s:s + 1] * vrow
    og = _gla_out_gate(oacc_ref[...], proj[:, 2 * GLA_DK + GLA_DV:], ng_ref[...])
    o_ref[...] = og.astype(F32)


def _gla_sample(proj, z, s0, ng):
    n = proj.shape[0]
    sb_ = SAMPLE_SEQ_BLOCK
    st = pl.BlockSpec((sb_,) + s0.shape[1:], lambda i: (i, 0, 0, 0))
    return pl.pallas_call(
        _gla_sample_kernel,
        grid=(n // sb_,),
        in_specs=[pl.BlockSpec((sb_, proj.shape[1]), lambda i: (i, 0)),
                  pl.BlockSpec((sb_, z.shape[1]), lambda i: (i, 0)), st,
                  pl.BlockSpec(ng.shape, lambda i: (0, 0))],
        out_specs=[pl.BlockSpec((sb_, GLA_DV), lambda i: (i, 0)), st],
        out_shape=[jax.ShapeDtypeStruct((n, GLA_DV), F32), jax.ShapeDtypeStruct(s0.shape, F32)],
        scratch_shapes=[pltpu.VMEM((sb_, GLA_DV), F32)],
        compiler_params=_cparams(("arbitrary",)),
        name="gla_sample",
    )(proj, z, s0, ng)


def _row(v):
    return v.reshape(1, -1)


def kernel(x_prompt, x_sample, cache_k, cache_v, state_gla, state_conv, attn_w_qkv, attn_b_qkv, attn_sinks, attn_w_o, gla_w_in, gla_w_a1, gla_w_a2, gla_b_a, gla_norm_g, gla_w_o, ffn_w_up, ffn_conv_w, ffn_conv_b, ffn_w_down, ln_mix_g, ln_mix_b, ln_ffn_g, ln_ffn_b):
    bsz, seq, d = x_prompt.shape
    nsmp = x_sample.shape[0]
    xp = x_prompt
    xs = x_sample.reshape(nsmp, d)
    zeros = lambda n: jnp.zeros((1, n), F32)
    ffn_params = (ffn_w_up.astype(BF16), ffn_w_down.astype(BF16), ffn_conv_w,
                  ffn_conv_b[:, None, :], ln_ffn_g[:, None, :], ln_ffn_b[:, None, :])

    kp_l, vp_l, ks_l, vs_l, sp_l, ss_l, cp_l, cs_l = [], [], [], [], [], [], [], []
    for i in range(DEPTH):
        j = i // 2
        g_mix, b_mix = _row(ln_mix_g[i]), _row(ln_mix_b[i])
        if i % 2 == 0:
            wqkv = attn_w_qkv[j].astype(BF16)
            bqkv = _row(attn_b_qkv[j])
            wo = attn_w_o[j].astype(BF16)
            xp, kp, vp = _attn_prompt(xp, wqkv, bqkv, attn_sinks[j], wo, g_mix, b_mix)
            kv_shape = (bsz, WINDOW, N_KV_HEADS, HEAD_DIM)
            kp_l.append(kp.reshape(kv_shape))
            vp_l.append(vp.reshape(kv_shape))

            wbuf = cache_k.shape[2]
            qkv_s = _linear(xs, wqkv, bqkv)
            o_s, k_s, v_s = _attn_sample(qkv_s, cache_k[j].reshape(nsmp, wbuf, LANES),
                                         cache_v[j].reshape(nsmp, wbuf, LANES), attn_sinks[j])
            xs = _linear_ln(o_s, wo, xs, g_mix, b_mix)
            ks_l.append(k_s.reshape(cache_k.shape[1:]))
            vs_l.append(v_s.reshape(cache_v.shape[1:]))
        else:
            win = gla_w_in[j].astype(BF16)
            wa1 = jnp.pad(gla_w_a1[j], ((0, 0), (0, LANES - GLA_RANK))).astype(BF16)
            wa2 = jnp.pad(gla_w_a2[j], ((0, LANES - GLA_RANK), (0, 0))).astype(BF16)
            ba = _row(gla_b_a[j])
            ng = _row(gla_norm_g[j])
            wo = gla_w_o[j].astype(BF16)
            xp, sp = _gla_prompt(xp, win, wa1, wa2, ba, ng, wo, g_mix, b_mix)
            sp_l.append(sp)

            proj_s = _linear(xs, win, zeros(win.shape[1]))
            t1 = _linear(xs, wa1, zeros(LANES))
            z_s = _linear(t1, wa2, ba)
            o_s, s_s = _gla_sample(proj_s, z_s, state_gla[j], ng)
            xs = _linear_ln(o_s, wo, xs, g_mix, b_mix)
            ss_l.append(s_s)

        xp2, cp = _ffn_prompt(xp.reshape(bsz * seq, d), seq, i, *ffn_params)
        xp = xp2.reshape(bsz, seq, d)
        cp_l.append(cp)
        xs, a_s = _ffn_sample(xs, state_conv[i, :, 0], state_conv[i, :, 1], i, *ffn_params)
        cs_l.append(jnp.stack([state_conv[i, :, 1], a_s], axis=1))

    return (xp, xs.reshape(x_sample.shape), jnp.stack(kp_l), jnp.stack(vp_l), jnp.stack(ks_l),
            jnp.stack(vs_l), jnp.stack(sp_l), jnp.stack(ss_l), jnp.stack(cp_l), jnp.stack(cs_l))
```

```python
import functools

import jax
import jax.numpy as jnp
from jax import lax
from jax.experimental import pallas as pl
from jax.experimental.pallas import tpu as pltpu

F32 = jnp.float32
BF16 = jnp.bfloat16

D_MODEL = 1024
DEPTH = 2
PAST_LEN = 16384
N_HEADS = 16
N_KV_HEADS = 2
HEAD_DIM = 64
GROUP = N_HEADS // N_KV_HEADS
WINDOW = 128
ROPE_THETA = 10000.0
ATTN_SCALE = HEAD_DIM ** -0.5
GLA_HEADS = 4
GLA_DK = D_MODEL // 2
GLA_DV = D_MODEL
GLA_DKH = GLA_DK // GLA_HEADS
GLA_DVH = GLA_DV // GLA_HEADS
GLA_RANK = 16
GLA_TAU = 16.0
GLA_CHUNK = 64
D_FF = 2816
CONV_W = 3
ALPHA = (2 * DEPTH) ** 0.25
LN_EPS = 1e-5
NORM_EPS = 1e-6
NEG_INF = -1e30

LANES = 128
SUBLANES = 8
VMEM_LIMIT_BYTES = 56 * 1024 * 1024

ATTN_TQ = 1024
FFN_TM = 1024
FFN_TF = 256
FFN_ROW_SPLITS = 4
GLA_TM = 512
GLA_ROW_SPLITS = 4
SAMPLE_SEQ_BLOCK = 8


def _cparams(sem):
    return pltpu.CompilerParams(dimension_semantics=sem, vmem_limit_bytes=VMEM_LIMIT_BYTES)


def _resident(shape):
    nd = len(shape)
    return pl.BlockSpec(shape, lambda *_: (0,) * nd, pipeline_mode=pl.Buffered(1))


def _resident_layer(stacked, layer):
    nd = stacked.ndim - 1
    return pl.BlockSpec((None,) + stacked.shape[1:], lambda *_: (layer,) + (0,) * nd,
                        pipeline_mode=pl.Buffered(1))


def _layer_norm(z, g, b):
    mu = jnp.mean(z, axis=-1, keepdims=True)
    zc = z - mu
    var = jnp.mean(zc * zc, axis=-1, keepdims=True)
    return zc * lax.rsqrt(var + LN_EPS) * g + b


def _gelu(c):
    return 0.5 * c * (1.0 + lax.erf(c * (2.0 ** -0.5)))


def _dot(a, b):
    return jnp.dot(a, b, preferred_element_type=F32)


def _dot_nt(a, b):
    return lax.dot_general(a, b, (((1,), (1,)), ((), ())), preferred_element_type=F32)


def _dot_tn(a, b):
    return lax.dot_general(a, b, (((0,), (0,)), ((), ())), preferred_element_type=F32)


def _rope_tables(pos):
    inv = 1.0 / (ROPE_THETA ** (jnp.arange(0, HEAD_DIM, 2, dtype=F32) / HEAD_DIM))
    ang = pos[:, None] * inv[None, :]
    cos, sin = jnp.cos(ang), jnp.sin(ang)
    zero = jnp.zeros_like(sin)
    cos_h = jnp.concatenate([cos, cos], -1)
    sa_h = jnp.concatenate([-sin, zero], -1)
    sb_h = jnp.concatenate([zero, sin], -1)
    two = lambda t: jnp.concatenate([t, t], -1)
    return two(cos_h), two(sa_h), two(sb_h)


def _rope_slab(s, cos, sa, sb):
    return (s * cos + pltpu.roll(s, LANES - HEAD_DIM // 2, 1) * sa
            + pltpu.roll(s, HEAD_DIM // 2, 1) * sb)


def _sink_softmax(s, sink):
    m = jnp.maximum(jnp.max(s, axis=-1, keepdims=True), sink)
    p = jnp.exp(s - m)
    den = jnp.sum(p, axis=-1, keepdims=True) + jnp.exp(sink - m)
    return p * (1.0 / den)


def _lane_split_pair(t):
    lo = lax.broadcasted_iota(jnp.int32, t.shape, 1) < HEAD_DIM
    tr = pltpu.roll(t, HEAD_DIM, 1)
    z = jnp.zeros_like(t)
    return (jnp.where(lo, t, z), jnp.where(lo, z, tr), jnp.where(lo, tr, z), jnp.where(lo, z, t))


def _attn_prompt_kernel(sinks_ref, x_ref, cos_ref, sa_ref, sb_ref, wqkv_ref, bqkv_ref, wo_ref,
                        g_ref, b_ref, y_ref, kn_ref, vn_ref, kprev_ref, vprev_ref):
    ti = pl.program_id(1)
    nq = N_HEADS * HEAD_DIM
    w = WINDOW

    @pl.when(ti == 0)
    def _():
        kprev_ref[...] = jnp.zeros_like(kprev_ref)
        vprev_ref[...] = jnp.zeros_like(vprev_ref)

    nblk = x_ref.shape[1] // w
    nslab = nq // LANES
    spg = nslab // N_KV_HEADS

    t = lax.broadcasted_iota(jnp.int32, (w, w), 0)
    s = lax.broadcasted_iota(jnp.int32, (w, w), 1)
    upper = s > t
    has_prev = ti > 0

    def project(blk):
        rows = slice(blk * w, (blk + 1) * w)
        x = x_ref[0, rows, :]
        qkv = _dot(x.astype(BF16), wqkv_ref[...]) + bqkv_ref[...]
        cos, sa, sb = cos_ref[rows, :], sa_ref[rows, :], sb_ref[rows, :]
        k = _rope_slab(qkv[:, nq:nq + LANES], cos, sa, sb)
        v = qkv[:, nq + LANES:nq + 2 * LANES]
        q = [(_rope_slab(qkv[:, j * LANES:(j + 1) * LANES], cos, sa, sb) * ATTN_SCALE).astype(BF16)
             for j in range(nslab)]
        return dict(x=x, q=q, k=k, v=v, ks=_lane_split_pair(k.astype(BF16)),
                    vs=_lane_split_pair(v.astype(BF16)))

    def finish(blk, st, pv):
        slabs = []
        for g in range(N_KV_HEADS):
            og = pv[g, 0] + pv[g, 1]
            slabs += [og[i * w:(i + 1) * w] for i in range(spg)]
        o = jnp.concatenate(slabs, axis=1).astype(BF16)
        z = ALPHA * st["x"] + _dot(o, wo_ref[...])
        y_ref[0, blk * w:(blk + 1) * w, :] = _layer_norm(z, g_ref[...], b_ref[...])

    units = [(g, par) for g in range(N_KV_HEADS) for par in range(2)]
    prev = dict(ks=_lane_split_pair(kprev_ref[...].astype(BF16)),
                vs=_lane_split_pair(vprev_ref[...].astype(BF16)))
    stages = {-1: prev, 0: project(0)}
    pvs = {}
    for blk in range(nblk):
        if blk + 1 < nblk:
            stages[blk + 1] = project(blk + 1)
        st, pst = stages[blk], stages[blk - 1]

        def scores(u):
            g, par = u
            qg = jnp.concatenate([st["q"][g * spg + i] for i in range(spg)], axis=0)
            kmat = jnp.concatenate([pst["ks"][2 * g + par], st["ks"][2 * g + par]], axis=0)
            return _dot_nt(qg, kmat)

        pv = {}
        sc_nxt = scores(units[0])
        for n, (g, par) in enumerate(units):
            sc = sc_nxt
            if n + 1 < len(units):
                sc_nxt = scores(units[n + 1])
            ps = []
            for i in range(spg):
                head = 2 * (g * spg + i) + par
                s_prev = sc[i * w:(i + 1) * w, :w]
                if blk == 0:
                    s_prev = jnp.where(has_prev, s_prev, NEG_INF)
                p = _sink_softmax(jnp.where(upper, s_prev, sc[i * w:(i + 1) * w, w:]), sinks_ref[head])
                zero = jnp.zeros_like(p)
                ps.append(jnp.concatenate([jnp.where(upper, p, zero), jnp.where(upper, zero, p)],
                                          axis=1).astype(BF16))
            vmat = jnp.concatenate([pst["vs"][2 * g + par], st["vs"][2 * g + par]], axis=0)
            pv[g, par] = _dot(jnp.concatenate(ps, axis=0), vmat)
        pvs[blk] = pv
        if blk >= 1:
            finish(blk - 1, stages[blk - 1], pvs[blk - 1])
    finish(nblk - 1, stages[nblk - 1], pvs[nblk - 1])

    k_last, v_last = stages[nblk - 1]["k"], stages[nblk - 1]["v"]
    kprev_ref[...] = k_last
    vprev_ref[...] = v_last

    @pl.when(ti == pl.num_programs(1) - 1)
    def _():
        kn_ref[0] = k_last
        vn_ref[0] = v_last


def _attn_prompt(x, wqkv, bqkv, sinks, wo, g, b):
    bsz, seq, d = x.shape
    tq = min(ATTN_TQ, seq)
    nqkv = wqkv.shape[1]
    cos, sa, sb = _rope_tables(jnp.arange(seq, dtype=F32))
    tab = pl.BlockSpec((tq, LANES), lambda bi, ti: (ti, 0))
    kv_out = pl.BlockSpec((1, WINDOW, LANES), lambda bi, ti: (bi, 0, 0))
    return pl.pallas_call(
        _attn_prompt_kernel,
        grid=(bsz, seq // tq),
        in_specs=[
            pl.BlockSpec(memory_space=pltpu.SMEM),
            pl.BlockSpec((1, tq, d), lambda bi, ti: (bi, ti, 0)),
            tab, tab, tab,
            _resident((d, nqkv)), _resident((1, nqkv)), _resident((d, d)),
            _resident((1, d)), _resident((1, d)),
        ],
        out_specs=[pl.BlockSpec((1, tq, d), lambda bi, ti: (bi, ti, 0)), kv_out, kv_out],
        out_shape=[jax.ShapeDtypeStruct((bsz, seq, d), F32),
                   jax.ShapeDtypeStruct((bsz, WINDOW, LANES), F32),
                   jax.ShapeDtypeStruct((bsz, WINDOW, LANES), F32)],
        scratch_shapes=[pltpu.VMEM((WINDOW, LANES), F32), pltpu.VMEM((WINDOW, LANES), F32)],
        compiler_params=_cparams(("arbitrary", "arbitrary")),
        name="attn_prompt",
    )(sinks, x, cos, sa, sb, wqkv, bqkv, wo, g, b)


def _conv_gate(a, a1, a2, u, cw, cb):
    c = cb + cw[0:1] * a2 + cw[1:2] * a1 + cw[2:3] * a
    return (_gelu(c) * u).astype(BF16)


def _ffn_prompt_kernel(x_ref, wup_ref, wd_ref, cw_ref, cb_ref, g_ref, b_ref,
                       y_ref, cs_ref, carry_ref, h_ref, *, tiles_per_seq):
    i = pl.program_id(0)
    pos = i % tiles_per_seq

    @pl.when(pos == 0)
    def _():
        carry_ref[...] = jnp.zeros_like(carry_ref)

    x = x_ref[...]
    xb = x.astype(BF16)
    tm = x.shape[0]
    dff = wd_ref.shape[0]
    tf = FFN_TF
    row = lax.broadcasted_iota(jnp.int32, (SUBLANES, tf), 0)
    for j in range(dff // tf):
        sl = slice(j * tf, (j + 1) * tf)
        a = _dot(xb, wup_ref[:, sl])
        u = _dot(xb, wup_ref[:, dff + j * tf:dff + (j + 1) * tf])
        prev = carry_ref[:, sl]
        p1 = prev[SUBLANES - 1:SUBLANES]
        p2 = prev[SUBLANES - 2:SUBLANES - 1]
        a1 = pltpu.roll(a, 1, 0)
        a2 = pltpu.roll(a, 2, 0)
        a1_top = jnp.where(row == 0, p1, a1[:SUBLANES])
        a2_top = jnp.where(row == 0, p2, jnp.where(row == 1, p1, a2[:SUBLANES]))
        a1 = jnp.concatenate([a1_top, a1[SUBLANES:]], axis=0)
        a2 = jnp.concatenate([a2_top, a2[SUBLANES:]], axis=0)
        carry_ref[:, sl] = a[tm - SUBLANES:tm]
        h_ref[:, sl] = _conv_gate(a, a1, a2, u, cw_ref[:, sl], cb_ref[:, sl])

    rt = tm // FFN_ROW_SPLITS
    for r in range(FFN_ROW_SPLITS):
        rows = slice(r * rt, (r + 1) * rt)
        z = ALPHA * x[rows] + _dot(h_ref[rows, :], wd_ref[...])
        y_ref[rows, :] = _layer_norm(z, g_ref[...], b_ref[...])

    @pl.when(pos == tiles_per_seq - 1)
    def _():
        cs_ref[0] = carry_ref[SUBLANES - (CONV_W - 1):SUBLANES, :]


def _ffn_prompt(x, seq, layer, wup, wd, cw, cb, g, b):
    n, d = x.shape
    tm = min(FFN_TM, seq)
    tiles_per_seq = seq // tm
    dff = wd.shape[1]
    return pl.pallas_call(
        functools.partial(_ffn_prompt_kernel, tiles_per_seq=tiles_per_seq),
        grid=(n // tm,),
        in_specs=[pl.BlockSpec((tm, d), lambda i: (i, 0))]
        + [_resident_layer(a, layer) for a in (wup, wd, cw, cb, g, b)],
        out_specs=[pl.BlockSpec((tm, d), lambda i: (i, 0)),
                   pl.BlockSpec((1, CONV_W - 1, dff), lambda i: (i // tiles_per_seq, 0, 0))],
        out_shape=[jax.ShapeDtypeStruct((n, d), F32),
                   jax.ShapeDtypeStruct((n // seq, CONV_W - 1, dff), F32)],
        scratch_shapes=[pltpu.VMEM((SUBLANES, dff), F32), pltpu.VMEM((tm, dff), BF16)],
        compiler_params=_cparams(("arbitrary",)),
        name="ffn_prompt",
    )(x, wup, wd, cw, cb, g, b)


def _ffn_sample_kernel(x_ref, s0_ref, s1_ref, wa_ref, wu_ref, wd_ref, cw_ref, cb_ref, g_ref, b_ref,
                       y_ref, a_ref, acc_ref):
    j = pl.program_id(0)
    x = x_ref[...]
    xb = x.astype(BF16)
    a = _dot(xb, wa_ref[...])
    u = _dot(xb, wu_ref[...])
    a_ref[...] = a
    h = _conv_gate(a, s1_ref[...], s0_ref[...], u, cw_ref[...], cb_ref[...])
    dn = _dot(h, wd_ref[...])

    @pl.when(j == 0)
    def _():
        acc_ref[...] = dn

    @pl.when(j > 0)
    def _():
        acc_ref[...] += dn

    @pl.when(j == pl.num_programs(0) - 1)
    def _():
        y_ref[...] = _layer_norm(ALPHA * x + acc_ref[...], g_ref[...], b_ref[...])


def _ffn_sample(x, s0, s1, layer, wup, wd, cw, cb, g, b):
    n, d = x.shape
    dff = wd.shape[1]
    tf = FFN_TF
    nchunk = dff // tf
    col = pl.BlockSpec((n, tf), lambda j: (0, j))
    return pl.pallas_call(
        _ffn_sample_kernel,
        grid=(nchunk,),
        in_specs=[
            pl.BlockSpec((n, d), lambda j: (0, 0)), col, col,
            pl.BlockSpec((None, d, tf), lambda j: (layer, 0, j)),
            pl.BlockSpec((None, d, tf), lambda j: (layer, 0, j + nchunk)),
            pl.BlockSpec((None, tf, d), lambda j: (layer, j, 0)),
            pl.BlockSpec((None, CONV_W, tf), lambda j: (layer, 0, j)),
            pl.BlockSpec((None, 1, tf), lambda j: (layer, 0, j)),
            pl.BlockSpec((None, 1, d), lambda j: (layer, 0, 0)),
            pl.BlockSpec((None, 1, d), lambda j: (layer, 0, 0)),
        ],
        out_specs=[pl.BlockSpec((n, d), lambda j: (0, 0)), col],
        out_shape=[jax.ShapeDtypeStruct((n, d), F32), jax.ShapeDtypeStruct((n, dff), F32)],
        scratch_shapes=[pltpu.VMEM((n, d), F32)],
        compiler_params=_cparams(("arbitrary",)),
        name="ffn_sample",
    )(x, s0, s1, wup, wup, wd, cw, cb, g, b)


def _gla_decay_logits(xb, wa1_ref, wa2_ref, ba_ref):
    t1 = _dot(xb, wa1_ref[...])
    z = _dot(t1.astype(BF16), wa2_ref[...]) + ba_ref[...]
    return jax.nn.log_sigmoid(z) * (1.0 / GLA_TAU)


def _gla_out_gate(o, gate, ng):
    outs = []
    for h in range(GLA_HEADS):
        oh = o[:, h * GLA_DVH:(h + 1) * GLA_DVH]
        ms = jnp.mean(oh * oh, axis=-1, keepdims=True)
        outs.append(oh * lax.rsqrt(ms + NORM_EPS) * ng)
    on = jnp.concatenate(outs, axis=1)
    return (on * jax.nn.silu(gate)).astype(BF16)


def _split3(t):
    hi = t.astype(BF16)
    r1 = t - hi.astype(F32)
    mid = r1.astype(BF16)
    lo = (r1 - mid.astype(F32)).astype(BF16)
    return hi, mid, lo


def _gla_prompt_kernel(x_ref, win_ref, wa1_ref, wa2_ref, ba_ref, ng_ref, wo_ref, g_ref, b_ref,
                       y_ref, so_ref, s_ref, proj_ref, u_ref, o_ref):
    ti = pl.program_id(1)
    ck = GLA_CHUNK

    @pl.when(ti == 0)
    def _():
        s_ref[...] = jnp.zeros_like(s_ref)

    x = x_ref[0]
    xb = x.astype(BF16)
    tm = x.shape[0]
    nck = tm // ck
    la = _gla_decay_logits(xb, wa1_ref, wa2_ref, ba_ref)
    c_v, c_g = 2 * GLA_DK, 2 * GLA_DK + GLA_DV
    proj_ref[:, :c_v] = _dot(xb, win_ref[:, :c_v])
    la_parts = _split3(la)

    r = lax.broadcasted_iota(jnp.int32, (ck, ck), 0)
    c = lax.broadcasted_iota(jnp.int32, (ck, ck), 1)
    causal = r >= c
    tril = causal.astype(BF16)

    tril3 = jnp.concatenate([tril] * 3 + [jnp.zeros_like(tril)], axis=1)
    zrows = jnp.zeros((ck, GLA_DK), BF16)
    bcs = [_dot(tril3, jnp.concatenate([part[ci * ck:(ci + 1) * ck] for part in la_parts] + [zrows],
                                       axis=0))
           for ci in range(nck)]

    proj_ref[:, c_v:c_g] = _dot(xb, win_ref[:, c_v:c_g])
    q_in, vb, att, dec = {}, {}, {}, {}
    for ci in range(nck):
        rows = slice(ci * ck, (ci + 1) * ck)
        bc = bcs[ci]
        for h in range(GLA_HEADS):
            bh = bc[:, h * GLA_DKH:(h + 1) * GLA_DKH]
            q = proj_ref[rows, h * GLA_DKH:(h + 1) * GLA_DKH] * (GLA_DKH ** -0.5)
            k = proj_ref[rows, GLA_DK + h * GLA_DKH:GLA_DK + (h + 1) * GLA_DKH]
            v = proj_ref[rows, 2 * GLA_DK + h * GLA_DVH:2 * GLA_DK + (h + 1) * GLA_DVH].astype(BF16)
            qi = (q * jnp.exp(bh)).astype(BF16)
            k_in = (k * jnp.exp(-bh)).astype(BF16)
            bl = bh[ck - 1:ck]
            kd = (k * jnp.exp(bl - bh)).astype(BF16)
            att[ci, h] = _dot_nt(qi, k_in)
            u_ref[ci, h] = _dot_tn(kd, v)
            q_in[ci, h], vb[ci, h], dec[ci, h] = qi, v, jnp.exp(bl)

    proj_ref[:, c_g:] = _dot(xb, win_ref[:, c_g:])
    state = [s_ref[h] for h in range(GLA_HEADS)]
    zpad_l = jnp.zeros((ck, ck), BF16)
    zpad_r = jnp.zeros((ck, GLA_DVH), BF16)
    for ci in range(nck):
        rows = slice(ci * ck, (ci + 1) * ck)
        for h in range(GLA_HEADS):
            am = jnp.where(causal, att[ci, h], 0.0).astype(BF16)
            lhs = jnp.concatenate([q_in[ci, h], am, zpad_l], axis=1)
            rhs = jnp.concatenate([state[h].astype(BF16), vb[ci, h], zpad_r], axis=0)
            o_ref[rows, h * GLA_DVH:(h + 1) * GLA_DVH] = _dot(lhs, rhs)
            dcol = jnp.transpose(jnp.broadcast_to(dec[ci, h], (GLA_DKH, GLA_DKH)))
            dfull = jnp.concatenate([dcol] * (GLA_DVH // GLA_DKH), axis=1)
            state[h] = dfull * state[h] + u_ref[ci, h]
    for h in range(GLA_HEADS):
        s_ref[h] = state[h]

    rt = tm // GLA_ROW_SPLITS
    for rs in range(GLA_ROW_SPLITS):
        rows = slice(rs * rt, (rs + 1) * rt)
        og = _gla_out_gate(o_ref[rows, :], proj_ref[rows, 2 * GLA_DK + GLA_DV:], ng_ref[...])
        z = ALPHA * x[rows] + _dot(og, wo_ref[...])
        y_ref[0, rows, :] = _layer_norm(z, g_ref[...], b_ref[...])

    @pl.when(ti == pl.num_programs(1) - 1)
    def _():
        so_ref[0] = s_ref[...]


def _gla_prompt(x, win, wa1, wa2, ba, ng, wo, g, b):
    bsz, seq, d = x.shape
    tm = min(GLA_TM, seq)
    nin = win.shape[1]
    st = (GLA_HEADS, GLA_DKH, GLA_DVH)
    return pl.pallas_call(
        _gla_prompt_kernel,
        grid=(bsz, seq // tm),
        in_specs=[
            pl.BlockSpec((1, tm, d), lambda bi, ti: (bi, ti, 0)),
            _resident(win.shape), _resident(wa1.shape), _resident(wa2.shape), _resident(ba.shape),
            _resident(ng.shape), _resident(wo.shape), _resident(g.shape), _resident(b.shape),
        ],
        out_specs=[pl.BlockSpec((1, tm, d), lambda bi, ti: (bi, ti, 0)),
                   pl.BlockSpec((1,) + st, lambda bi, ti: (bi, 0, 0, 0))],
        out_shape=[jax.ShapeDtypeStruct((bsz, seq, d), F32),
                   jax.ShapeDtypeStruct((bsz,) + st, F32)],
        scratch_shapes=[pltpu.VMEM(st, F32), pltpu.VMEM((tm, nin), F32),
                        pltpu.VMEM((tm // GLA_CHUNK,) + st, F32), pltpu.VMEM((tm, GLA_DV), F32)],
        compiler_params=_cparams(("arbitrary", "arbitrary")),
        name="gla_prompt",
    )(x, win, wa1, wa2, ba, ng, wo, g, b)


def _linear_kernel(x_ref, w_ref, b_ref, o_ref):
    o_ref[...] = _dot(x_ref[...].astype(BF16), w_ref[...]) + b_ref[...]


def _linear(x, w, bias):
    n, kdim = x.shape
    nout = w.shape[1]
    tn = next(t for t in (512, 256, 128) if nout % t == 0)
    return pl.pallas_call(
        _linear_kernel,
        grid=(nout // tn,),
        in_specs=[pl.BlockSpec((n, kdim), lambda j: (0, 0)),
                  pl.BlockSpec((kdim, tn), lambda j: (0, j)),
                  pl.BlockSpec((1, tn), lambda j: (0, j))],
        out_specs=pl.BlockSpec((n, tn), lambda j: (0, j)),
        out_shape=jax.ShapeDtypeStruct((n, nout), F32),
        compiler_params=_cparams(("arbitrary",)),
        name="sample_linear",
    )(x, w, bias)


def _linear_ln_kernel(h_ref, w_ref, r_ref, g_ref, b_ref, o_ref):
    z = ALPHA * r_ref[...] + _dot(h_ref[...].astype(BF16), w_ref[...])
    o_ref[...] = _layer_norm(z, g_ref[...], b_ref[...])


def _linear_ln(h, w, resid, g, b):
    n, d = resid.shape
    full = lambda a: pl.BlockSpec(a.shape, lambda i: (0,) * a.ndim)
    return pl.pallas_call(
        _linear_ln_kernel,
        grid=(1,),
        in_specs=[full(h), full(w), full(resid), full(g), full(b)],
        out_specs=pl.BlockSpec((n, d), lambda i: (0, 0)),
        out_shape=jax.ShapeDtypeStruct((n, d), F32),
        compiler_params=_cparams(("arbitrary",)),
        name="sample_linear_ln",
    )(h, w, resid, g, b)


def _attn_sample_kernel(q2_ref, kv_ref, ck_ref, cv_ref, cos_ref, sa_ref, sb_ref, sink_ref,
                        o2_ref, kn_ref, vn_ref):
    nseq, wbuf, _ = ck_ref.shape
    cos, sa, sb = cos_ref[...], sa_ref[...], sb_ref[...]
    q2 = (_rope_slab(q2_ref[...], cos, sa, sb) * ATTN_SCALE).astype(BF16)
    kv = kv_ref[...]
    k_all = _rope_slab(kv[:, :LANES], cos, sa, sb)
    v_all = kv[:, LANES:]
    last = lax.broadcasted_iota(jnp.int32, (wbuf, LANES), 0) == wbuf - 1
    sink = sink_ref[...]

    k_new, v_new = [], []
    for s in range(nseq):
        k_new.append(jnp.where(last, k_all[s:s + 1], pltpu.roll(ck_ref[s], wbuf - 1, 0)))
        v_new.append(jnp.where(last, v_all[s:s + 1], pltpu.roll(cv_ref[s], wbuf - 1, 0)))
        kn_ref[s] = k_new[s]
        vn_ref[s] = v_new[s]
    sc = [_dot_nt(q2[s * N_HEADS:(s + 1) * N_HEADS], k_new[s].astype(BF16)) for s in range(nseq)]
    ps = [_sink_softmax(sc[s], sink).astype(BF16) for s in range(nseq)]
    for s in range(nseq):
        o2_ref[s * N_HEADS:(s + 1) * N_HEADS, :] = _dot(ps[s], v_new[s].astype(BF16))


def _attn_sample(qkv, ck, cv, sinks):
    n = qkv.shape[0]
    wbuf = ck.shape[1]
    sb_ = SAMPLE_SEQ_BLOCK
    nq = N_HEADS * HEAD_DIM
    cos, sa, sb = _rope_tables(jnp.full((1,), PAST_LEN, dtype=F32))
    one = lambda a: pl.BlockSpec(a.shape, lambda i: (0,) * a.ndim)
    sink_col = sinks.reshape(N_HEADS, 1)
    q = qkv[:, :nq].reshape(n, N_HEADS, HEAD_DIM)
    zero = jnp.zeros_like(q)
    first_group = (jnp.arange(N_HEADS) < GROUP)[None, :, None]
    q2 = jnp.concatenate([jnp.where(first_group, q, zero), jnp.where(first_group, zero, q)], axis=-1)
    q2 = q2.reshape(n * N_HEADS, LANES)
    cache = pl.BlockSpec((sb_, wbuf, LANES), lambda i: (i, 0, 0))
    rows16 = pl.BlockSpec((sb_ * N_HEADS, LANES), lambda i: (i, 0))
    o2, k_s, v_s = pl.pallas_call(
        _attn_sample_kernel,
        grid=(n // sb_,),
        in_specs=[rows16, pl.BlockSpec((sb_, 2 * LANES), lambda i: (i, 0)), cache, cache,
                  one(cos), one(sa), one(sb), one(sink_col)],
        out_specs=[rows16, cache, cache],
        out_shape=[jax.ShapeDtypeStruct((n * N_HEADS, LANES), F32),
                   jax.ShapeDtypeStruct(ck.shape, F32), jax.ShapeDtypeStruct(cv.shape, F32)],
        compiler_params=_cparams(("arbitrary",)),
        name="attn_sample",
    )(q2, qkv[:, nq:], ck, cv, cos, sa, sb, sink_col)
    o2 = o2.reshape(n, N_HEADS, N_KV_HEADS, HEAD_DIM)
    o = jnp.where(first_group, o2[:, :, 0, :], o2[:, :, 1, :])
    return o.reshape(n, nq), k_s, v_s


def _gla_sample_kernel(proj_ref, z_ref, s0_ref, ng_ref, o_ref, sn_ref, oacc_ref):
    proj = proj_ref[...]
    la = jax.nn.log_sigmoid(z_ref[...]) * (1.0 / GLA_TAU)
    nseq = proj.shape[0]
    pad = jnp.zeros((LANES - nseq, GLA_DKH), F32)

    def cols(t):
        return jnp.transpose(jnp.concatenate([t, pad], axis=0))

    for h in range(GLA_HEADS):
        b = la[:, h * GLA_DKH:(h + 1) * GLA_DKH]
        q = proj[:, h * GLA_DKH:(h + 1) * GLA_DKH] * (GLA_DKH ** -0.5)
        k = proj[:, GLA_DK + h * GLA_DKH:GLA_DK + (h + 1) * GLA_DKH]
        v = proj[:, 2 * GLA_DK + h * GLA_DVH:2 * GLA_DK + (h + 1) * GLA_DVH]
        eb = jnp.exp(b)
        q_in = q * eb
        k_in = k * jnp.exp(-b)
        att = jnp.sum(q_in * k_in, axis=-1, keepdims=True)
        kd = k * jnp.exp(b - b)
        qc, kc, ec = cols(q_in), cols(kd), cols(eb)
        for s in range(nseq):
            s_old = s0_ref[s, h]
            vrow = v[s:s + 1]
            o_row = att[s:s + 1] * vrow + jnp.sum(qc[:, s:s + 1] * s_old, axis=0, keepdims=True)
            oacc_ref[s:s + 1, h * GLA_DVH:(h + 1) * GLA_DVH] = o_row
            sn_ref[s, h] = ec[:, s:s + 1] * s_old + kc[:, s:s + 1] * vrow
    og = _gla_out_gate(oacc_ref[...], proj[:, 2 * GLA_DK + GLA_DV:], ng_ref[...])
    o_ref[...] = og.astype(F32)


def _gla_sample(proj, z, s0, ng):
    n = proj.shape[0]
    sb_ = SAMPLE_SEQ_BLOCK
    st = pl.BlockSpec((sb_,) + s0.shape[1:], lambda i: (i, 0, 0, 0))
    return pl.pallas_call(
        _gla_sample_kernel,
        grid=(n // sb_,),
        in_specs=[pl.BlockSpec((sb_, proj.shape[1]), lambda i: (i, 0)),
                  pl.BlockSpec((sb_, z.shape[1]), lambda i: (i, 0)), st,
                  pl.BlockSpec(ng.shape, lambda i: (0, 0))],
        out_specs=[pl.BlockSpec((sb_, GLA_DV), lambda i: (i, 0)), st],
        out_shape=[jax.ShapeDtypeStruct((n, GLA_DV), F32), jax.ShapeDtypeStruct(s0.shape, F32)],
        scratch_shapes=[pltpu.VMEM((sb_, GLA_DV), F32)],
        compiler_params=_cparams(("arbitrary",)),
        name="gla_sample",
    )(proj, z, s0, ng)


def _row(v):
    return v.reshape(1, -1)


def kernel(x_prompt, x_sample, cache_k, cache_v, state_gla, state_conv, attn_w_qkv, attn_b_qkv, attn_sinks, attn_w_o, gla_w_in, gla_w_a1, gla_w_a2, gla_b_a, gla_norm_g, gla_w_o, ffn_w_up, ffn_conv_w, ffn_conv_b, ffn_w_down, ln_mix_g, ln_mix_b, ln_ffn_g, ln_ffn_b):
    bsz, seq, d = x_prompt.shape
    nsmp = x_sample.shape[0]
    xp = x_prompt
    xs = x_sample.reshape(nsmp, d)
    zeros = lambda n: jnp.zeros((1, n), F32)
    ffn_params = (ffn_w_up.astype(BF16), ffn_w_down.astype(BF16), ffn_conv_w,
                  ffn_conv_b[:, None, :], ln_ffn_g[:, None, :], ln_ffn_b[:, None, :])

    kp_l, vp_l, ks_l, vs_l, sp_l, ss_l, cp_l, cs_l = [], [], [], [], [], [], [], []
    for i in range(DEPTH):
        j = i // 2
        g_mix, b_mix = _row(ln_mix_g[i]), _row(ln_mix_b[i])
        if i % 2 == 0:
            wqkv = attn_w_qkv[j].astype(BF16)
            bqkv = _row(attn_b_qkv[j])
            wo = attn_w_o[j].astype(BF16)
            xp, kp, vp = _attn_prompt(xp, wqkv, bqkv, attn_sinks[j], wo, g_mix, b_mix)
            kv_shape = (bsz, WINDOW, N_KV_HEADS, HEAD_DIM)
            kp_l.append(kp.reshape(kv_shape))
            vp_l.append(vp.reshape(kv_shape))

            wbuf = cache_k.shape[2]
            qkv_s = _linear(xs, wqkv, bqkv)
            o_s, k_s, v_s = _attn_sample(qkv_s, cache_k[j].reshape(nsmp, wbuf, LANES),
                                         cache_v[j].reshape(nsmp, wbuf, LANES), attn_sinks[j])
            xs = _linear_ln(o_s, wo, xs, g_mix, b_mix)
            ks_l.append(k_s.reshape(cache_k.shape[1:]))
            vs_l.append(v_s.reshape(cache_v.shape[1:]))
        else:
            win = gla_w_in[j].astype(BF16)
            wa1 = jnp.pad(gla_w_a1[j], ((0, 0), (0, LANES - GLA_RANK))).astype(BF16)
            wa2 = jnp.pad(gla_w_a2[j], ((0, LANES - GLA_RANK), (0, 0))).astype(BF16)
            ba = _row(gla_b_a[j])
            ng = _row(gla_norm_g[j])
            wo = gla_w_o[j].astype(BF16)
            xp, sp = _gla_prompt(xp, win, wa1, wa2, ba, ng, wo, g_mix, b_mix)
            sp_l.append(sp)

            proj_s = _linear(xs, win, zeros(win.shape[1]))
            t1 = _linear(xs, wa1, zeros(LANES))
            z_s = _linear(t1, wa2, ba)
            o_s, s_s = _gla_sample(proj_s, z_s, state_gla[j], ng)
            xs = _linear_ln(o_s, wo, xs, g_mix, b_mix)
            ss_l.append(s_s)

        xp2, cp = _ffn_prompt(xp.reshape(bsz * seq, d), seq, i, *ffn_params)
        xp = xp2.reshape(bsz, seq, d)
        cp_l.append(cp)
        xs, a_s = _ffn_sample(xs, state_conv[i, :, 0], state_conv[i, :, 1], i, *ffn_params)
        cs_l.append(jnp.stack([state_conv[i, :, 1], a_s], axis=1))

    return (xp, xs.reshape(x_sample.shape), jnp.stack(kp_l), jnp.stack(vp_l), jnp.stack(ks_l),
            jnp.stack(vs_l), jnp.stack(sp_l), jnp.stack(ss_l), jnp.stack(cp_l), jnp.stack(cs_l))
```

```python
import functools

import jax
import jax.numpy as jnp
from jax import lax
from jax.experimental import pallas as pl
from jax.experimental.pallas import tpu as pltpu

F32 = jnp.float32
BF16 = jnp.bfloat16

D_MODEL = 1024
DEPTH = 2
PAST_LEN = 16384
N_HEADS = 16
N_KV_HEADS = 2
HEAD_DIM = 64
GROUP = N_HEADS // N_KV_HEADS
WINDOW = 128
ROPE_THETA = 10000.0
ATTN_SCALE = HEAD_DIM ** -0.5
GLA_HEADS = 4
GLA_DK = D_MODEL // 2
GLA_DV = D_MODEL
GLA_DKH = GLA_DK // GLA_HEADS
GLA_DVH = GLA_DV // GLA_HEADS
GLA_RANK = 16
GLA_TAU = 16.0
GLA_CHUNK = 64
D_FF = 2816
CONV_W = 3
ALPHA = (2 * DEPTH) ** 0.25
LN_EPS = 1e-5
NORM_EPS = 1e-6
NEG_INF = -1e30

LANES = 128
SUBLANES = 8
VMEM_LIMIT_BYTES = 56 * 1024 * 1024

ATTN_TQ = 1024
FFN_TM = 1024
FFN_TF = 256
FFN_ROW_SPLITS = 4
GLA_TM = 512
GLA_ROW_SPLITS = 4
SAMPLE_SEQ_BLOCK = 8


def _cparams(sem):
    return pltpu.CompilerParams(dimension_semantics=sem, vmem_limit_bytes=VMEM_LIMIT_BYTES)


def _resident(shape):
    nd = len(shape)
    return pl.BlockSpec(shape, lambda *_: (0,) * nd, pipeline_mode=pl.Buffered(1))


def _resident_layer(stacked, layer):
    nd = stacked.ndim - 1
    return pl.BlockSpec((None,) + stacked.shape[1:], lambda *_: (layer,) + (0,) * nd,
                        pipeline_mode=pl.Buffered(1))


def _layer_norm(z, g, b):
    mu = jnp.mean(z, axis=-1, keepdims=True)
    zc = z - mu
    var = jnp.mean(zc * zc, axis=-1, keepdims=True)
    return zc * lax.rsqrt(var + LN_EPS) * g + b


def _gelu(c):
    return 0.5 * c * (1.0 + lax.erf(c * (2.0 ** -0.5)))


def _dot(a, b):
    return jnp.dot(a, b, preferred_element_type=F32)


def _dot_nt(a, b):
    return lax.dot_general(a, b, (((1,), (1,)), ((), ())), preferred_element_type=F32)


def _dot_tn(a, b):
    return lax.dot_general(a, b, (((0,), (0,)), ((), ())), preferred_element_type=F32)


def _rope_tables(pos):
    inv = 1.0 / (ROPE_THETA ** (jnp.arange(0, HEAD_DIM, 2, dtype=F32) / HEAD_DIM))
    ang = pos[:, None] * inv[None, :]
    cos, sin = jnp.cos(ang), jnp.sin(ang)
    zero = jnp.zeros_like(sin)
    cos_h = jnp.concatenate([cos, cos], -1)
    sa_h = jnp.concatenate([-sin, zero], -1)
    sb_h = jnp.concatenate([zero, sin], -1)
    two = lambda t: jnp.concatenate([t, t], -1)
    return two(cos_h), two(sa_h), two(sb_h)


def _rope_slab(s, cos, sa, sb):
    return (s * cos + pltpu.roll(s, LANES - HEAD_DIM // 2, 1) * sa
            + pltpu.roll(s, HEAD_DIM // 2, 1) * sb)


def _sink_softmax(s, sink):
    m = jnp.maximum(jnp.max(s, axis=-1, keepdims=True), sink)
    p = jnp.exp(s - m)
    den = jnp.sum(p, axis=-1, keepdims=True) + jnp.exp(sink - m)
    return p * (1.0 / den)


def _lane_split_pair(t):
    lo = lax.broadcasted_iota(jnp.int32, t.shape, 1) < HEAD_DIM
    tr = pltpu.roll(t, HEAD_DIM, 1)
    z = jnp.zeros_like(t)
    return (jnp.where(lo, t, z), jnp.where(lo, z, tr), jnp.where(lo, tr, z), jnp.where(lo, z, t))


def _attn_prompt_kernel(sinks_ref, x_ref, cos_ref, sa_ref, sb_ref, wqkv_ref, bqkv_ref, wo_ref,
                        g_ref, b_ref, y_ref, kn_ref, vn_ref, kprev_ref, vprev_ref):
    ti = pl.program_id(1)
    nq = N_HEADS * HEAD_DIM
    w = WINDOW

    @pl.when(ti == 0)
    def _():
        kprev_ref[...] = jnp.zeros_like(kprev_ref)
        vprev_ref[...] = jnp.zeros_like(vprev_ref)

    nblk = x_ref.shape[1] // w
    nslab = nq // LANES
    spg = nslab // N_KV_HEADS

    t = lax.broadcasted_iota(jnp.int32, (w, w), 0)
    s = lax.broadcasted_iota(jnp.int32, (w, w), 1)
    upper = s > t
    has_prev = ti > 0

    def project(blk):
        rows = slice(blk * w, (blk + 1) * w)
        x = x_ref[0, rows, :]
        qkv = _dot(x.astype(BF16), wqkv_ref[...]) + bqkv_ref[...]
        cos, sa, sb = cos_ref[rows, :], sa_ref[rows, :], sb_ref[rows, :]
        k = _rope_slab(qkv[:, nq:nq + LANES], cos, sa, sb)
        v = qkv[:, nq + LANES:nq + 2 * LANES]
        q = [(_rope_slab(qkv[:, j * LANES:(j + 1) * LANES], cos, sa, sb) * ATTN_SCALE).astype(BF16)
             for j in range(nslab)]
        return dict(x=x, q=q, k=k, v=v, ks=_lane_split_pair(k.astype(BF16)),
                    vs=_lane_split_pair(v.astype(BF16)))

    def finish(blk, st, pv):
        slabs = []
        for g in range(N_KV_HEADS):
            og = pv[g, 0] + pv[g, 1]
            slabs += [og[i * w:(i + 1) * w] for i in range(spg)]
        o = jnp.concatenate(slabs, axis=1).astype(BF16)
        z = ALPHA * st["x"] + _dot(o, wo_ref[...])
        y_ref[0, blk * w:(blk + 1) * w, :] = _layer_norm(z, g_ref[...], b_ref[...])

    units = [(g, par) for g in range(N_KV_HEADS) for par in range(2)]
    prev = dict(ks=_lane_split_pair(kprev_ref[...].astype(BF16)),
                vs=_lane_split_pair(vprev_ref[...].astype(BF16)))
    stages = {-1: prev, 0: project(0)}
    pvs = {}
    for blk in range(nblk):
        if blk + 1 < nblk:
            stages[blk + 1] = project(blk + 1)
        st, pst = stages[blk], stages[blk - 1]

        def scores(u):
            g, par = u
            qg = jnp.concatenate([st["q"][g * spg + i] for i in range(spg)], axis=0)
            kmat = jnp.concatenate([pst["ks"][2 * g + par], st["ks"][2 * g + par]], axis=0)
            return _dot_nt(qg, kmat)

        pv = {}
        sc_nxt = scores(units[0])
        for n, (g, par) in enumerate(units):
            sc = sc_nxt
            if n + 1 < len(units):
                sc_nxt = scores(units[n + 1])
            ps = []
            for i in range(spg):
                head = 2 * (g * spg + i) + par
                s_prev = sc[i * w:(i + 1) * w, :w]
                if blk == 0:
                    s_prev = jnp.where(has_prev, s_prev, NEG_INF)
                p = _sink_softmax(jnp.where(upper, s_prev, sc[i * w:(i + 1) * w, w:]), sinks_ref[head])
                zero = jnp.zeros_like(p)
                ps.append(jnp.concatenate([jnp.where(upper, p, zero), jnp.where(upper, zero, p)],
                                          axis=1).astype(BF16))
            vmat = jnp.concatenate([pst["vs"][2 * g + par], st["vs"][2 * g + par]], axis=0)
            pv[g, par] = _dot(jnp.concatenate(ps, axis=0), vmat)
        pvs[blk] = pv
        if blk >= 1:
            finish(blk - 1, stages[blk - 1], pvs[blk - 1])
    finish(nblk - 1, stages[nblk - 1], pvs[nblk - 1])

    k_last, v_last = stages[nblk - 1]["k"], stages[nblk - 1]["v"]
    kprev_ref[...] = k_last
    vprev_ref[...] = v_last

    @pl.when(ti == pl.num_programs(1) - 1)
    def _():
        kn_ref[0] = k_last
        vn_ref[0] = v_last


def _attn_prompt(x, wqkv, bqkv, sinks, wo, g, b):
    bsz, seq, d = x.shape
    tq = min(ATTN_TQ, seq)
    nqkv = wqkv.shape[1]
    cos, sa, sb = _rope_tables(jnp.arange(seq, dtype=F32))
    tab = pl.BlockSpec((tq, LANES), lambda bi, ti: (ti, 0))
    kv_out = pl.BlockSpec((1, WINDOW, LANES), lambda bi, ti: (bi, 0, 0))
    return pl.pallas_call(
        _attn_prompt_kernel,
        grid=(bsz, seq // tq),
        in_specs=[
            pl.BlockSpec(memory_space=pltpu.SMEM),
            pl.BlockSpec((1, tq, d), lambda bi, ti: (bi, ti, 0)),
            tab, tab, tab,
            _resident((d, nqkv)), _resident((1, nqkv)), _resident((d, d)),
            _resident((1, d)), _resident((1, d)),
        ],
        out_specs=[pl.BlockSpec((1, tq, d), lambda bi, ti: (bi, ti, 0)), kv_out, kv_out],
        out_shape=[jax.ShapeDtypeStruct((bsz, seq, d), F32),
                   jax.ShapeDtypeStruct((bsz, WINDOW, LANES), F32),
                   jax.ShapeDtypeStruct((bsz, WINDOW, LANES), F32)],
        scratch_shapes=[pltpu.VMEM((WINDOW, LANES), F32), pltpu.VMEM((WINDOW, LANES), F32)],
        compiler_params=_cparams(("arbitrary", "arbitrary")),
        name="attn_prompt",
    )(sinks, x, cos, sa, sb, wqkv, bqkv, wo, g, b)


def _conv_gate(a, a1, a2, u, cw, cb):
    c = cb + cw[0:1] * a2 + cw[1:2] * a1 + cw[2:3] * a
    return (_gelu(c) * u).astype(BF16)


def _ffn_prompt_kernel(x_ref, wa_ref, wu_ref, wd_ref, cw_ref, cb_ref, g_ref, b_ref,
                       y_ref, cs_ref, carry_ref, h_ref, *, tiles_per_seq):
    i = pl.program_id(0)
    pos = i % tiles_per_seq

    @pl.when(pos == 0)
    def _():
        carry_ref[...] = jnp.zeros_like(carry_ref)

    x = x_ref[...]
    xb = x.astype(BF16)
    tm = x.shape[0]
    dff = wd_ref.shape[0]
    tf = FFN_TF
    row = lax.broadcasted_iota(jnp.int32, (SUBLANES, tf), 0)
    for j in range(dff // tf):
        sl = slice(j * tf, (j + 1) * tf)
        a = _dot(xb, wa_ref[:, sl])
        u = _dot(xb, wu_ref[:, sl])
        prev = carry_ref[:, sl]
        p1 = prev[SUBLANES - 1:SUBLANES]
        p2 = prev[SUBLANES - 2:SUBLANES - 1]
        a1 = pltpu.roll(a, 1, 0)
        a2 = pltpu.roll(a, 2, 0)
        a1_top = jnp.where(row == 0, p1, a1[:SUBLANES])
        a2_top = jnp.where(row == 0, p2, jnp.where(row == 1, p1, a2[:SUBLANES]))
        a1 = jnp.concatenate([a1_top, a1[SUBLANES:]], axis=0)
        a2 = jnp.concatenate([a2_top, a2[SUBLANES:]], axis=0)
        carry_ref[:, sl] = a[tm - SUBLANES:tm]
        h_ref[:, sl] = _conv_gate(a, a1, a2, u, cw_ref[:, sl], cb_ref[:, sl])

    rt = tm // FFN_ROW_SPLITS
    for r in range(FFN_ROW_SPLITS):
        rows = slice(r * rt, (r + 1) * rt)
        z = ALPHA * x[rows] + _dot(h_ref[rows, :], wd_ref[...])
        y_ref[rows, :] = _layer_norm(z, g_ref[...], b_ref[...])

    @pl.when(pos == tiles_per_seq - 1)
    def _():
        cs_ref[0] = carry_ref[SUBLANES - (CONV_W - 1):SUBLANES, :]


def _ffn_prompt(x, seq, layer, wa, wu, wd, cw, cb, g, b):
    n, d = x.shape
    tm = min(FFN_TM, seq)
    tiles_per_seq = seq // tm
    dff = wd.shape[0]
    return pl.pallas_call(
        functools.partial(_ffn_prompt_kernel, tiles_per_seq=tiles_per_seq),
        grid=(n // tm,),
        in_specs=[pl.BlockSpec((tm, d), lambda i: (i, 0))]
        + [_resident(a.shape) for a in (wa, wu, wd)]
        + [_resident_layer(a, layer) for a in (cw, cb, g, b)],
        out_specs=[pl.BlockSpec((tm, d), lambda i: (i, 0)),
                   pl.BlockSpec((1, CONV_W - 1, dff), lambda i: (i // tiles_per_seq, 0, 0))],
        out_shape=[jax.ShapeDtypeStruct((n, d), F32),
                   jax.ShapeDtypeStruct((n // seq, CONV_W - 1, dff), F32)],
        scratch_shapes=[pltpu.VMEM((SUBLANES, dff), F32), pltpu.VMEM((tm, dff), BF16)],
        compiler_params=_cparams(("arbitrary",)),
        name="ffn_prompt",
    )(x, wa, wu, wd, cw, cb, g, b)


def _ffn_sample_kernel(x_ref, s0_ref, s1_ref, wa_ref, wu_ref, wd_ref, cw_ref, cb_ref, g_ref, b_ref,
                       y_ref, a_ref, wab_ref, wub_ref, wdb_ref, acc_ref):
    j = pl.program_id(0)
    x = x_ref[...]
    xb = x.astype(BF16)
    wa, wu, wd = wa_ref[...].astype(BF16), wu_ref[...].astype(BF16), wd_ref[...].astype(BF16)
    wab_ref[...] = wa
    wub_ref[...] = wu
    wdb_ref[...] = wd
    a = _dot(xb, wa)
    u = _dot(xb, wu)
    a_ref[...] = a
    h = _conv_gate(a, s1_ref[...], s0_ref[...], u, cw_ref[...], cb_ref[...])
    dn = _dot(h, wd)

    @pl.when(j == 0)
    def _():
        acc_ref[...] = dn

    @pl.when(j > 0)
    def _():
        acc_ref[...] += dn

    @pl.when(j == pl.num_programs(0) - 1)
    def _():
        y_ref[...] = _layer_norm(ALPHA * x + acc_ref[...], g_ref[...], b_ref[...])


def _ffn_sample(x, s0, s1, layer, wup, wd, cw, cb, g, b):
    n, d = x.shape
    dff = wd.shape[1]
    tf = FFN_TF
    nchunk = dff // tf
    col = pl.BlockSpec((n, tf), lambda j: (0, j))
    return pl.pallas_call(
        _ffn_sample_kernel,
        grid=(nchunk,),
        in_specs=[
            pl.BlockSpec((n, d), lambda j: (0, 0)), col, col,
            pl.BlockSpec((None, d, tf), lambda j: (layer, 0, j)),
            pl.BlockSpec((None, d, tf), lambda j: (layer, 0, j + nchunk)),
            pl.BlockSpec((None, tf, d), lambda j: (layer, j, 0)),
            pl.BlockSpec((None, CONV_W, tf), lambda j: (layer, 0, j)),
            pl.BlockSpec((None, 1, tf), lambda j: (layer, 0, j)),
            pl.BlockSpec((None, 1, d), lambda j: (layer, 0, 0)),
            pl.BlockSpec((None, 1, d), lambda j: (layer, 0, 0)),
        ],
        out_specs=[pl.BlockSpec((n, d), lambda j: (0, 0)), col,
                   pl.BlockSpec((d, tf), lambda j: (0, j)), pl.BlockSpec((d, tf), lambda j: (0, j)),
                   pl.BlockSpec((tf, d), lambda j: (j, 0))],
        out_shape=[jax.ShapeDtypeStruct((n, d), F32), jax.ShapeDtypeStruct((n, dff), F32),
                   jax.ShapeDtypeStruct((d, dff), BF16), jax.ShapeDtypeStruct((d, dff), BF16),
                   jax.ShapeDtypeStruct((dff, d), BF16)],
        scratch_shapes=[pltpu.VMEM((n, d), F32)],
        compiler_params=_cparams(("arbitrary",)),
        name="ffn_sample",
    )(x, s0, s1, wup, wup, wd, cw, cb, g, b)


def _gla_decay_logits(xb, wa1_ref, wa2_ref, ba_ref):
    t1 = _dot(xb, wa1_ref[...])
    z = _dot(t1.astype(BF16), wa2_ref[...]) + ba_ref[...]
    return jax.nn.log_sigmoid(z) * (1.0 / GLA_TAU)


def _gla_out_gate(o, gate, ng):
    outs = []
    for h in range(GLA_HEADS):
        oh = o[:, h * GLA_DVH:(h + 1) * GLA_DVH]
        ms = jnp.mean(oh * oh, axis=-1, keepdims=True)
        outs.append(oh * lax.rsqrt(ms + NORM_EPS) * ng)
    on = jnp.concatenate(outs, axis=1)
    return (on * jax.nn.silu(gate)).astype(BF16)


def _split3(t):
    hi = t.astype(BF16)
    r1 = t - hi.astype(F32)
    mid = r1.astype(BF16)
    lo = (r1 - mid.astype(F32)).astype(BF16)
    return hi, mid, lo


def _gla_prompt_kernel(x_ref, win_ref, wa1_ref, wa2_ref, ba_ref, ng_ref, wo_ref, g_ref, b_ref,
                       y_ref, so_ref, s_ref, proj_ref, u_ref, o_ref):
    ti = pl.program_id(1)
    ck = GLA_CHUNK

    @pl.when(ti == 0)
    def _():
        s_ref[...] = jnp.zeros_like(s_ref)

    x = x_ref[0]
    xb = x.astype(BF16)
    tm = x.shape[0]
    nck = tm // ck
    la = _gla_decay_logits(xb, wa1_ref, wa2_ref, ba_ref)
    c_v, c_g = 2 * GLA_DK, 2 * GLA_DK + GLA_DV
    proj_ref[:, :c_v] = _dot(xb, win_ref[:, :c_v])
    la_parts = _split3(la)

    r = lax.broadcasted_iota(jnp.int32, (ck, ck), 0)
    c = lax.broadcasted_iota(jnp.int32, (ck, ck), 1)
    causal = r >= c
    tril = causal.astype(BF16)

    tril3 = jnp.concatenate([tril] * 3 + [jnp.zeros_like(tril)], axis=1)
    zrows = jnp.zeros((ck, GLA_DK), BF16)
    bcs = [_dot(tril3, jnp.concatenate([part[ci * ck:(ci + 1) * ck] for part in la_parts] + [zrows],
                                       axis=0))
           for ci in range(nck)]

    proj_ref[:, c_v:c_g] = _dot(xb, win_ref[:, c_v:c_g])
    q_in, vb, att, dec = {}, {}, {}, {}
    for ci in range(nck):
        rows = slice(ci * ck, (ci + 1) * ck)
        bc = bcs[ci]
        for h in range(GLA_HEADS):
            bh = bc[:, h * GLA_DKH:(h + 1) * GLA_DKH]
            q = proj_ref[rows, h * GLA_DKH:(h + 1) * GLA_DKH] * (GLA_DKH ** -0.5)
            k = proj_ref[rows, GLA_DK + h * GLA_DKH:GLA_DK + (h + 1) * GLA_DKH]
            v = proj_ref[rows, 2 * GLA_DK + h * GLA_DVH:2 * GLA_DK + (h + 1) * GLA_DVH].astype(BF16)
            qi = (q * jnp.exp(bh)).astype(BF16)
            k_in = (k * jnp.exp(-bh)).astype(BF16)
            bl = bh[ck - 1:ck]
            kd = (k * jnp.exp(bl - bh)).astype(BF16)
            att[ci, h] = _dot_nt(qi, k_in)
            u_ref[ci, h] = _dot_tn(kd, v)
            q_in[ci, h], vb[ci, h], dec[ci, h] = qi, v, jnp.exp(bl)

    proj_ref[:, c_g:] = _dot(xb, win_ref[:, c_g:])
    state = [s_ref[h] for h in range(GLA_HEADS)]
    zpad_l = jnp.zeros((ck, ck), BF16)
    zpad_r = jnp.zeros((ck, GLA_DVH), BF16)
    for ci in range(nck):
        rows = slice(ci * ck, (ci + 1) * ck)
        for h in range(GLA_HEADS):
            am = jnp.where(causal, att[ci, h], 0.0).astype(BF16)
            lhs = jnp.concatenate([q_in[ci, h], am, zpad_l], axis=1)
            rhs = jnp.concatenate([state[h].astype(BF16), vb[ci, h], zpad_r], axis=0)
            o_ref[rows, h * GLA_DVH:(h + 1) * GLA_DVH] = _dot(lhs, rhs)
            dcol = jnp.transpose(jnp.broadcast_to(dec[ci, h], (GLA_DKH, GLA_DKH)))
            dfull = jnp.concatenate([dcol] * (GLA_DVH // GLA_DKH), axis=1)
            state[h] = dfull * state[h] + u_ref[ci, h]
    for h in range(GLA_HEADS):
        s_ref[h] = state[h]

    rt = tm // GLA_ROW_SPLITS
    for rs in range(GLA_ROW_SPLITS):
        rows = slice(rs * rt, (rs + 1) * rt)
        og = _gla_out_gate(o_ref[rows, :], proj_ref[rows, 2 * GLA_DK + GLA_DV:], ng_ref[...])
        z = ALPHA * x[rows] + _dot(og, wo_ref[...])
        y_ref[0, rows, :] = _layer_norm(z, g_ref[...], b_ref[...])

    @pl.when(ti == pl.num_programs(1) - 1)
    def _():
        so_ref[0] = s_ref[...]


def _gla_prompt(x, win, wa1, wa2, ba, ng, wo, g, b):
    bsz, seq, d = x.shape
    tm = min(GLA_TM, seq)
    nin = win.shape[1]
    st = (GLA_HEADS, GLA_DKH, GLA_DVH)
    return pl.pallas_call(
        _gla_prompt_kernel,
        grid=(bsz, seq // tm),
        in_specs=[
            pl.BlockSpec((1, tm, d), lambda bi, ti: (bi, ti, 0)),
            _resident(win.shape), _resident(wa1.shape), _resident(wa2.shape), _resident(ba.shape),
            _resident(ng.shape), _resident(wo.shape), _resident(g.shape), _resident(b.shape),
        ],
        out_specs=[pl.BlockSpec((1, tm, d), lambda bi, ti: (bi, ti, 0)),
                   pl.BlockSpec((1,) + st, lambda bi, ti: (bi, 0, 0, 0))],
        out_shape=[jax.ShapeDtypeStruct((bsz, seq, d), F32),
                   jax.ShapeDtypeStruct((bsz,) + st, F32)],
        scratch_shapes=[pltpu.VMEM(st, F32), pltpu.VMEM((tm, nin), F32),
                        pltpu.VMEM((tm // GLA_CHUNK,) + st, F32), pltpu.VMEM((tm, GLA_DV), F32)],
        compiler_params=_cparams(("arbitrary", "arbitrary")),
        name="gla_prompt",
    )(x, win, wa1, wa2, ba, ng, wo, g, b)


def _linear_kernel(x_ref, w_ref, b_ref, o_ref):
    o_ref[...] = _dot(x_ref[...].astype(BF16), w_ref[...]) + b_ref[...]


def _linear_cast_kernel(x_ref, w_ref, b_ref, o_ref, wb_ref):
    wb = w_ref[...].astype(BF16)
    wb_ref[...] = wb
    o_ref[...] = _dot(x_ref[...].astype(BF16), wb) + b_ref[...]


def _linear(x, w, bias):
    n, kdim = x.shape
    nout = w.shape[1]
    tn = next(t for t in (512, 256, 128) if nout % t == 0)
    emit = w.dtype != BF16
    wspec = pl.BlockSpec((kdim, tn), lambda j: (0, j))
    out_specs = [pl.BlockSpec((n, tn), lambda j: (0, j))]
    out_shape = [jax.ShapeDtypeStruct((n, nout), F32)]
    if emit:
        out_specs.append(wspec)
        out_shape.append(jax.ShapeDtypeStruct(w.shape, BF16))
    outs = pl.pallas_call(
        _linear_cast_kernel if emit else _linear_kernel,
        grid=(nout // tn,),
        in_specs=[pl.BlockSpec((n, kdim), lambda j: (0, 0)), wspec,
                  pl.BlockSpec((1, tn), lambda j: (0, j))],
        out_specs=out_specs,
        out_shape=out_shape,
        compiler_params=_cparams(("arbitrary",)),
        name="sample_linear",
    )(x, w, bias)
    return tuple(outs) if emit else outs[0]


def _linear_ln_kernel(h_ref, w_ref, r_ref, g_ref, b_ref, o_ref, wb_ref):
    wb = w_ref[...].astype(BF16)
    wb_ref[...] = wb
    z = ALPHA * r_ref[...] + _dot(h_ref[...].astype(BF16), wb)
    o_ref[...] = _layer_norm(z, g_ref[...], b_ref[...])


def _linear_ln(h, w, resid, g, b):
    n, d = resid.shape
    full = lambda a: pl.BlockSpec(a.shape, lambda i: (0,) * a.ndim)
    return pl.pallas_call(
        _linear_ln_kernel,
        grid=(1,),
        in_specs=[full(h), full(w), full(resid), full(g), full(b)],
        out_specs=[pl.BlockSpec((n, d), lambda i: (0, 0)), full(w)],
        out_shape=[jax.ShapeDtypeStruct((n, d), F32), jax.ShapeDtypeStruct(w.shape, BF16)],
        compiler_params=_cparams(("arbitrary",)),
        name="sample_linear_ln",
    )(h, w, resid, g, b)


def _attn_sample_kernel(q2_ref, kv_ref, ck_ref, cv_ref, cos_ref, sa_ref, sb_ref, sink_ref,
                        o2_ref, kn_ref, vn_ref):
    nseq, wbuf, _ = ck_ref.shape
    cos, sa, sb = cos_ref[...], sa_ref[...], sb_ref[...]
    q2 = (_rope_slab(q2_ref[...], cos, sa, sb) * ATTN_SCALE).astype(BF16)
    kv = kv_ref[...]
    k_all = _rope_slab(kv[:, :LANES], cos, sa, sb)
    v_all = kv[:, LANES:]
    last = lax.broadcasted_iota(jnp.int32, (wbuf, LANES), 0) == wbuf - 1
    sink = sink_ref[...]

    k_new, v_new = [], []
    for s in range(nseq):
        k_new.append(jnp.where(last, k_all[s:s + 1], pltpu.roll(ck_ref[s], wbuf - 1, 0)))
        v_new.append(jnp.where(last, v_all[s:s + 1], pltpu.roll(cv_ref[s], wbuf - 1, 0)))
        kn_ref[s] = k_new[s]
        vn_ref[s] = v_new[s]
    sc = [_dot_nt(q2[s * N_HEADS:(s + 1) * N_HEADS], k_new[s].astype(BF16)) for s in range(nseq)]
    ps = [_sink_softmax(sc[s], sink).astype(BF16) for s in range(nseq)]
    for s in range(nseq):
        o2_ref[s * N_HEADS:(s + 1) * N_HEADS, :] = _dot(ps[s], v_new[s].astype(BF16))


def _attn_sample(qkv, ck, cv, sinks):
    n = qkv.shape[0]
    wbuf = ck.shape[1]
    sb_ = SAMPLE_SEQ_BLOCK
    nq = N_HEADS * HEAD_DIM
    cos, sa, sb = _rope_tables(jnp.full((1,), PAST_LEN, dtype=F32))
    one = lambda a: pl.BlockSpec(a.shape, lambda i: (0,) * a.ndim)
    sink_col = sinks.reshape(N_HEADS, 1)
    q = qkv[:, :nq].reshape(n, N_HEADS, HEAD_DIM)
    zero = jnp.zeros_like(q)
    first_group = (jnp.arange(N_HEADS) < GROUP)[None, :, None]
    q2 = jnp.concatenate([jnp.where(first_group, q, zero), jnp.where(first_group, zero, q)], axis=-1)
    q2 = q2.reshape(n * N_HEADS, LANES)
    cache = pl.BlockSpec((sb_, wbuf, LANES), lambda i: (i, 0, 0))
    rows16 = pl.BlockSpec((sb_ * N_HEADS, LANES), lambda i: (i, 0))
    o2, k_s, v_s = pl.pallas_call(
        _attn_sample_kernel,
        grid=(n // sb_,),
        in_specs=[rows16, pl.BlockSpec((sb_, 2 * LANES), lambda i: (i, 0)), cache, cache,
                  one(cos), one(sa), one(sb), one(sink_col)],
        out_specs=[rows16, cache, cache],
        out_shape=[jax.ShapeDtypeStruct((n * N_HEADS, LANES), F32),
                   jax.ShapeDtypeStruct(ck.shape, F32), jax.ShapeDtypeStruct(cv.shape, F32)],
        compiler_params=_cparams(("arbitrary",)),
        name="attn_sample",
    )(q2, qkv[:, nq:], ck, cv, cos, sa, sb, sink_col)
    o2 = o2.reshape(n, N_HEADS, N_KV_HEADS, HEAD_DIM)
    o = jnp.where(first_group, o2[:, :, 0, :], o2[:, :, 1, :])
    return o.reshape(n, nq), k_s, v_s


def _gla_sample_kernel(proj_ref, z_ref, s0_ref, ng_ref, o_ref, sn_ref, oacc_ref):
    proj = proj_ref[...]
    la = jax.nn.log_sigmoid(z_ref[...]) * (1.0 / GLA_TAU)
    nseq = proj.shape[0]
    pad = jnp.zeros((LANES - nseq, GLA_DKH), F32)

    def cols(t):
        return jnp.transpose(jnp.concatenate([t, pad], axis=0))

    for h in range(GLA_HEADS):
        b = la[:, h * GLA_DKH:(h + 1) * GLA_DKH]
        q = proj[:, h * GLA_DKH:(h + 1) * GLA_DKH] * (GLA_DKH ** -0.5)
        k = proj[:, GLA_DK + h * GLA_DKH:GLA_DK + (h + 1) * GLA_DKH]
        v = proj[:, 2 * GLA_DK + h * GLA_DVH:2 * GLA_DK + (h + 1) * GLA_DVH]
        eb = jnp.exp(b)
        q_in = q * eb
        k_in = k * jnp.exp(-b)
        att = jnp.sum(q_in * k_in, axis=-1, keepdims=True)
        kd = k * jnp.exp(b - b)
        qc, kc, ec = cols(q_in), cols(kd), cols(eb)
        for s in range(nseq):
            s_old = s0_ref[s, h]
            vrow = v[s:s + 1]
            o_row = att[s:s + 1] * vrow + jnp.sum(qc[:, s:s + 1] * s_old, axis=0, keepdims=True)
            oacc_ref[s:s + 1, h * GLA_DVH:(h + 1) * GLA_DVH] = o_row
            sn_ref[s, h] = ec[:, s:s + 1] * s_old + kc[:, s:s + 1] * vrow
    og = _gla_out_gate(oacc_ref[...], proj[:, 2 * GLA_DK + GLA_DV:], ng_ref[...])
    o_ref[...] = og.astype(F32)


def _gla_sample(proj, z, s0, ng):
    n = proj.shape[0]
    sb_ = SAMPLE_SEQ_BLOCK
    st = pl.BlockSpec((sb_,) + s0.shape[1:], lambda i: (i, 0, 0, 0))
    return pl.pallas_call(
        _gla_sample_kernel,
        grid=(n // sb_,),
        in_specs=[pl.BlockSpec((sb_, proj.shape[1]), lambda i: (i, 0)),
                  pl.BlockSpec((sb_, z.shape[1]), lambda i: (i, 0)), st,
                  pl.BlockSpec(ng.shape, lambda i: (0, 0))],
        out_specs=[pl.BlockSpec((sb_, GLA_DV), lambda i: (i, 0)), st],
        out_shape=[jax.ShapeDtypeStruct((n, GLA_DV), F32), jax.ShapeDtypeStruct(s0.shape, F32)],
        scratch_shapes=[pltpu.VMEM((sb_, GLA_DV), F32)],
        compiler_params=_cparams(("arbitrary",)),
        name="gla_sample",
    )(proj, z, s0, ng)


def _row(v):
    return v.reshape(1, -1)


def kernel(x_prompt, x_sample, cache_k, cache_v, state_gla, state_conv, attn_w_qkv, attn_b_qkv, attn_sinks, attn_w_o, gla_w_in, gla_w_a1, gla_w_a2, gla_b_a, gla_norm_g, gla_w_o, ffn_w_up, ffn_conv_w, ffn_conv_b, ffn_w_down, ln_mix_g, ln_mix_b, ln_ffn_g, ln_ffn_b):
    bsz, seq, d = x_prompt.shape
    nsmp = x_sample.shape[0]
    xp = x_prompt
    xs = x_sample.reshape(nsmp, d)
    zeros = lambda n: jnp.zeros((1, n), F32)
    ffn_small = (ffn_conv_w, ffn_conv_b[:, None, :], ln_ffn_g[:, None, :], ln_ffn_b[:, None, :])

    kp_l, vp_l, ks_l, vs_l, sp_l, ss_l, cp_l, cs_l = [], [], [], [], [], [], [], []
    for i in range(DEPTH):
        j = i // 2
        g_mix, b_mix = _row(ln_mix_g[i]), _row(ln_mix_b[i])
        if i % 2 == 0:
            bqkv = _row(attn_b_qkv[j])
            wbuf = cache_k.shape[2]
            qkv_s, wqkv = _linear(xs, attn_w_qkv[j], bqkv)
            o_s, k_s, v_s = _attn_sample(qkv_s, cache_k[j].reshape(nsmp, wbuf, LANES),
                                         cache_v[j].reshape(nsmp, wbuf, LANES), attn_sinks[j])
            xs, wo = _linear_ln(o_s, attn_w_o[j], xs, g_mix, b_mix)
            ks_l.append(k_s.reshape(cache_k.shape[1:]))
            vs_l.append(v_s.reshape(cache_v.shape[1:]))

            xp, kp, vp = _attn_prompt(xp, wqkv, bqkv, attn_sinks[j], wo, g_mix, b_mix)
            kv_shape = (bsz, WINDOW, N_KV_HEADS, HEAD_DIM)
            kp_l.append(kp.reshape(kv_shape))
            vp_l.append(vp.reshape(kv_shape))
        else:
            wa1 = jnp.pad(gla_w_a1[j], ((0, 0), (0, LANES - GLA_RANK))).astype(BF16)
            wa2 = jnp.pad(gla_w_a2[j], ((0, LANES - GLA_RANK), (0, 0))).astype(BF16)
            ba = _row(gla_b_a[j])
            ng = _row(gla_norm_g[j])
            proj_s, win = _linear(xs, gla_w_in[j], zeros(gla_w_in.shape[2]))
            t1 = _linear(xs, wa1, zeros(LANES))
            z_s = _linear(t1, wa2, ba)
            o_s, s_s = _gla_sample(proj_s, z_s, state_gla[j], ng)
            xs, wo = _linear_ln(o_s, gla_w_o[j], xs, g_mix, b_mix)
            ss_l.append(s_s)

            xp, sp = _gla_prompt(xp, win, wa1, wa2, ba, ng, wo, g_mix, b_mix)
            sp_l.append(sp)

        xs, a_s, wa, wu, wd = _ffn_sample(xs, state_conv[i, :, 0], state_conv[i, :, 1], i,
                                          ffn_w_up, ffn_w_down, *ffn_small)
        cs_l.append(jnp.stack([state_conv[i, :, 1], a_s], axis=1))
        xp2, cp = _ffn_prompt(xp.reshape(bsz * seq, d), seq, i, wa, wu, wd, *ffn_small)
        xp = xp2.reshape(bsz, seq, d)
        cp_l.append(cp)

    return (xp, xs.reshape(x_sample.shape), jnp.stack(kp_l), jnp.stack(vp_l), jnp.stack(ks_l),
            jnp.stack(vs_l), jnp.stack(sp_l), jnp.stack(ss_l), jnp.stack(cp_l), jnp.stack(cs_l))
```

```python
import functools

import jax
import jax.numpy as jnp
from jax import lax
from jax.experimental import pallas as pl
from jax.experimental.pallas import tpu as pltpu

F32 = jnp.float32
BF16 = jnp.bfloat16

D_MODEL = 1024
DEPTH = 2
PAST_LEN = 16384
N_HEADS = 16
N_KV_HEADS = 2
HEAD_DIM = 64
GROUP = N_HEADS // N_KV_HEADS
WINDOW = 128
ROPE_THETA = 10000.0
ATTN_SCALE = HEAD_DIM ** -0.5
GLA_HEADS = 4
GLA_DK = D_MODEL // 2
GLA_DV = D_MODEL
GLA_DKH = GLA_DK // GLA_HEADS
GLA_DVH = GLA_DV // GLA_HEADS
GLA_RANK = 16
GLA_TAU = 16.0
GLA_CHUNK = 64
D_FF = 2816
CONV_W = 3
ALPHA = (2 * DEPTH) ** 0.25
LN_EPS = 1e-5
NORM_EPS = 1e-6
NEG_INF = -1e30

LANES = 128
SUBLANES = 8
VMEM_LIMIT_BYTES = 56 * 1024 * 1024

ATTN_TQ = 1024
FFN_TM = 1024
FFN_TF = 256
FFN_ROW_SPLITS = 4
GLA_TM = 512
GLA_ROW_SPLITS = 4
SAMPLE_SEQ_BLOCK = 8


def _cparams(sem):
    return pltpu.CompilerParams(dimension_semantics=sem, vmem_limit_bytes=VMEM_LIMIT_BYTES)


def _resident(shape):
    nd = len(shape)
    return pl.BlockSpec(shape, lambda *_: (0,) * nd, pipeline_mode=pl.Buffered(1))


def _resident_layer(stacked, layer):
    nd = stacked.ndim - 1
    return pl.BlockSpec((None,) + stacked.shape[1:], lambda *_: (layer,) + (0,) * nd,
                        pipeline_mode=pl.Buffered(1))


def _layer_norm(z, g, b):
    mu = jnp.mean(z, axis=-1, keepdims=True)
    zc = z - mu
    var = jnp.mean(zc * zc, axis=-1, keepdims=True)
    return zc * lax.rsqrt(var + LN_EPS) * g + b


def _gelu(c):
    return 0.5 * c * (1.0 + lax.erf(c * (2.0 ** -0.5)))


def _dot(a, b):
    return jnp.dot(a, b, preferred_element_type=F32)


def _dot_nt(a, b):
    return lax.dot_general(a, b, (((1,), (1,)), ((), ())), preferred_element_type=F32)


def _dot_tn(a, b):
    return lax.dot_general(a, b, (((0,), (0,)), ((), ())), preferred_element_type=F32)


def _rope_tables(pos):
    inv = 1.0 / (ROPE_THETA ** (jnp.arange(0, HEAD_DIM, 2, dtype=F32) / HEAD_DIM))
    ang = pos[:, None] * inv[None, :]
    cos, sin = jnp.cos(ang), jnp.sin(ang)
    zero = jnp.zeros_like(sin)
    cos_h = jnp.concatenate([cos, cos], -1)
    sa_h = jnp.concatenate([-sin, zero], -1)
    sb_h = jnp.concatenate([zero, sin], -1)
    two = lambda t: jnp.concatenate([t, t], -1)
    return two(cos_h), two(sa_h), two(sb_h)


def _rope_slab(s, cos, sa, sb):
    return (s * cos + pltpu.roll(s, LANES - HEAD_DIM // 2, 1) * sa
            + pltpu.roll(s, HEAD_DIM // 2, 1) * sb)


HALF = HEAD_DIM // 2


def _paired_perm():
    base = jnp.arange(HALF)
    return jnp.concatenate([base, HEAD_DIM + base, HALF + base, HEAD_DIM + HALF + base])


def _rope_tables_paired(pos):
    inv = 1.0 / (ROPE_THETA ** (jnp.arange(0, HEAD_DIM, 2, dtype=F32) / HEAD_DIM))
    ang = pos[:, None] * inv[None, :]
    cos, sin = jnp.cos(ang), jnp.sin(ang)
    return (jnp.concatenate([cos] * 4, -1), jnp.concatenate([-sin, -sin, sin, sin], -1))


def _rope_paired(s, cos, sin_signed):
    return s * cos + pltpu.roll(s, HEAD_DIM, 1) * sin_signed


def _paired_split(t):
    lane = lax.broadcasted_iota(jnp.int32, t.shape, 1)
    on_a = (lane % HEAD_DIM) < HALF
    z = jnp.zeros_like(t)
    return (jnp.where(on_a, t, z), jnp.where(on_a, z, pltpu.roll(t, HALF, 1)),
            jnp.where(on_a, pltpu.roll(t, LANES - HALF, 1), z), jnp.where(on_a, z, t))


def _paired_to_standard(t):
    lane = lax.broadcasted_iota(jnp.int32, t.shape, 1)
    keep = (lane < HALF) | (lane >= LANES - HALF)
    return jnp.where(keep, t, jnp.where(lane < HEAD_DIM, pltpu.roll(t, LANES - HALF, 1),
                                        pltpu.roll(t, HALF, 1)))


def _sink_softmax(s, sink):
    m = jnp.maximum(jnp.max(s, axis=-1, keepdims=True), sink)
    p = jnp.exp(s - m)
    den = jnp.sum(p, axis=-1, keepdims=True) + jnp.exp(sink - m)
    return p * (1.0 / den)


def _lane_split_pair(t):
    lo = lax.broadcasted_iota(jnp.int32, t.shape, 1) < HEAD_DIM
    tr = pltpu.roll(t, HEAD_DIM, 1)
    z = jnp.zeros_like(t)
    return (jnp.where(lo, t, z), jnp.where(lo, z, tr), jnp.where(lo, tr, z), jnp.where(lo, z, t))


def _attn_prompt_kernel(sinks_ref, x_ref, cos_ref, sin_ref, wqkv_ref, bqkv_ref, wo_ref,
                        g_ref, b_ref, y_ref, kn_ref, vn_ref, kprev_ref, vprev_ref):
    ti = pl.program_id(1)
    nq = N_HEADS * HEAD_DIM
    w = WINDOW

    @pl.when(ti == 0)
    def _():
        kprev_ref[...] = jnp.zeros_like(kprev_ref)
        vprev_ref[...] = jnp.zeros_like(vprev_ref)

    nblk = x_ref.shape[1] // w
    nslab = nq // LANES
    spg = nslab // N_KV_HEADS

    t = lax.broadcasted_iota(jnp.int32, (w, w), 0)
    s = lax.broadcasted_iota(jnp.int32, (w, w), 1)
    upper = s > t
    has_prev = ti > 0

    def project(blk):
        rows = slice(blk * w, (blk + 1) * w)
        x = x_ref[0, rows, :]
        qkv = _dot(x.astype(BF16), wqkv_ref[...]) + bqkv_ref[...]
        cos, sin = cos_ref[rows, :], sin_ref[rows, :]
        k = _rope_paired(qkv[:, nq:nq + LANES], cos, sin)
        v = qkv[:, nq + LANES:nq + 2 * LANES]
        q = [(_rope_paired(qkv[:, j * LANES:(j + 1) * LANES], cos, sin) * ATTN_SCALE).astype(BF16)
             for j in range(nslab)]
        return dict(x=x, q=q, k=k, v=v, ks=_paired_split(k.astype(BF16)),
                    vs=_lane_split_pair(v.astype(BF16)))

    def finish(blk, st, pv):
        slabs = []
        for g in range(N_KV_HEADS):
            og = pv[g, 0] + pv[g, 1]
            slabs += [og[i * w:(i + 1) * w] for i in range(spg)]
        o = jnp.concatenate(slabs, axis=1).astype(BF16)
        z = ALPHA * st["x"] + _dot(o, wo_ref[...])
        y_ref[0, blk * w:(blk + 1) * w, :] = _layer_norm(z, g_ref[...], b_ref[...])

    units = [(g, par) for g in range(N_KV_HEADS) for par in range(2)]
    prev = dict(ks=_paired_split(kprev_ref[...].astype(BF16)),
                vs=_lane_split_pair(vprev_ref[...].astype(BF16)))
    stages = {-1: prev, 0: project(0)}
    pvs = {}
    for blk in range(nblk):
        if blk + 1 < nblk:
            stages[blk + 1] = project(blk + 1)
        st, pst = stages[blk], stages[blk - 1]

        def scores(u):
            g, par = u
            qg = jnp.concatenate([st["q"][g * spg + i] for i in range(spg)], axis=0)
            kmat = jnp.concatenate([pst["ks"][2 * g + par], st["ks"][2 * g + par]], axis=0)
            return _dot_nt(qg, kmat)

        pv = {}
        sc_nxt = scores(units[0])
        for n, (g, par) in enumerate(units):
            sc = sc_nxt
            if n + 1 < len(units):
                sc_nxt = scores(units[n + 1])
            ps = []
            for i in range(spg):
                head = 2 * (g * spg + i) + par
                s_prev = sc[i * w:(i + 1) * w, :w]
                if blk == 0:
                    s_prev = jnp.where(has_prev, s_prev, NEG_INF)
                p = _sink_softmax(jnp.where(upper, s_prev, sc[i * w:(i + 1) * w, w:]), sinks_ref[head])
                zero = jnp.zeros_like(p)
                ps.append(jnp.concatenate([jnp.where(upper, p, zero), jnp.where(upper, zero, p)],
                                          axis=1).astype(BF16))
            vmat = jnp.concatenate([pst["vs"][2 * g + par], st["vs"][2 * g + par]], axis=0)
            pv[g, par] = _dot(jnp.concatenate(ps, axis=0), vmat)
        pvs[blk] = pv
        if blk >= 1:
            finish(blk - 1, stages[blk - 1], pvs[blk - 1])
    finish(nblk - 1, stages[nblk - 1], pvs[nblk - 1])

    k_last, v_last = stages[nblk - 1]["k"], stages[nblk - 1]["v"]
    kprev_ref[...] = k_last
    vprev_ref[...] = v_last

    @pl.when(ti == pl.num_programs(1) - 1)
    def _():
        kn_ref[0] = _paired_to_standard(k_last)
        vn_ref[0] = v_last


def _attn_prompt(x, wqkv, bqkv, sinks, wo, g, b):
    bsz, seq, d = x.shape
    tq = min(ATTN_TQ, seq)
    nqkv = wqkv.shape[1]
    nq = N_HEADS * HEAD_DIM
    perm = _paired_perm()
    cols = jnp.concatenate([s * LANES + perm for s in range((nq + LANES) // LANES)]
                           + [jnp.arange(nq + LANES, nqkv)])
    wqkv, bqkv = wqkv[:, cols], bqkv[:, cols]
    cos, sin = _rope_tables_paired(jnp.arange(seq, dtype=F32))
    tab = pl.BlockSpec((tq, LANES), lambda bi, ti: (ti, 0))
    kv_out = pl.BlockSpec((1, WINDOW, LANES), lambda bi, ti: (bi, 0, 0))
    return pl.pallas_call(
        _attn_prompt_kernel,
        grid=(bsz, seq // tq),
        in_specs=[
            pl.BlockSpec(memory_space=pltpu.SMEM),
            pl.BlockSpec((1, tq, d), lambda bi, ti: (bi, ti, 0)),
            tab, tab,
            _resident((d, nqkv)), _resident((1, nqkv)), _resident((d, d)),
            _resident((1, d)), _resident((1, d)),
        ],
        out_specs=[pl.BlockSpec((1, tq, d), lambda bi, ti: (bi, ti, 0)), kv_out, kv_out],
        out_shape=[jax.ShapeDtypeStruct((bsz, seq, d), F32),
                   jax.ShapeDtypeStruct((bsz, WINDOW, LANES), F32),
                   jax.ShapeDtypeStruct((bsz, WINDOW, LANES), F32)],
        scratch_shapes=[pltpu.VMEM((WINDOW, LANES), F32), pltpu.VMEM((WINDOW, LANES), F32)],
        compiler_params=_cparams(("arbitrary", "arbitrary")),
        name="attn_prompt",
    )(sinks, x, cos, sin, wqkv, bqkv, wo, g, b)


def _conv_gate(a, a1, a2, u, cw, cb):
    c = cb + cw[0:1] * a2 + cw[1:2] * a1 + cw[2:3] * a
    return (_gelu(c) * u).astype(BF16)


def _ffn_prompt_kernel(x_ref, wa_ref, wu_ref, wd_ref, cw_ref, cb_ref, g_ref, b_ref,
                       y_ref, cs_ref, carry_ref, h_ref, *, tiles_per_seq):
    i = pl.program_id(0)
    pos = i % tiles_per_seq

    @pl.when(pos == 0)
    def _():
        carry_ref[...] = jnp.zeros_like(carry_ref)

    x = x_ref[...]
    xb = x.astype(BF16)
    tm = x.shape[0]
    dff = wd_ref.shape[0]
    tf = FFN_TF
    row = lax.broadcasted_iota(jnp.int32, (SUBLANES, tf), 0)
    for j in range(dff // tf):
        sl = slice(j * tf, (j + 1) * tf)
        a = _dot(xb, wa_ref[:, sl])
        u = _dot(xb, wu_ref[:, sl])
        prev = carry_ref[:, sl]
        p1 = prev[SUBLANES - 1:SUBLANES]
        p2 = prev[SUBLANES - 2:SUBLANES - 1]
        a1 = pltpu.roll(a, 1, 0)
        a2 = pltpu.roll(a, 2, 0)
        a1_top = jnp.where(row == 0, p1, a1[:SUBLANES])
        a2_top = jnp.where(row == 0, p2, jnp.where(row == 1, p1, a2[:SUBLANES]))
        a1 = jnp.concatenate([a1_top, a1[SUBLANES:]], axis=0)
        a2 = jnp.concatenate([a2_top, a2[SUBLANES:]], axis=0)
        carry_ref[:, sl] = a[tm - SUBLANES:tm]
        h_ref[:, sl] = _conv_gate(a, a1, a2, u, cw_ref[:, sl], cb_ref[:, sl])

    rt = tm // FFN_ROW_SPLITS
    for r in range(FFN_ROW_SPLITS):
        rows = slice(r * rt, (r + 1) * rt)
        z = ALPHA * x[rows] + _dot(h_ref[rows, :], wd_ref[...])
        y_ref[rows, :] = _layer_norm(z, g_ref[...], b_ref[...])

    @pl.when(pos == tiles_per_seq - 1)
    def _():
        cs_ref[0] = carry_ref[SUBLANES - (CONV_W - 1):SUBLANES, :]


def _ffn_prompt(x, seq, layer, wa, wu, wd, cw, cb, g, b):
    n, d = x.shape
    tm = min(FFN_TM, seq)
    tiles_per_seq = seq // tm
    dff = wd.shape[0]
    return pl.pallas_call(
        functools.partial(_ffn_prompt_kernel, tiles_per_seq=tiles_per_seq),
        grid=(n // tm,),
        in_specs=[pl.BlockSpec((tm, d), lambda i: (i, 0))]
        + [_resident(a.shape) for a in (wa, wu, wd)]
        + [_resident_layer(a, layer) for a in (cw, cb, g, b)],
        out_specs=[pl.BlockSpec((tm, d), lambda i: (i, 0)),
                   pl.BlockSpec((1, CONV_W - 1, dff), lambda i: (i // tiles_per_seq, 0, 0))],
        out_shape=[jax.ShapeDtypeStruct((n, d), F32),
                   jax.ShapeDtypeStruct((n // seq, CONV_W - 1, dff), F32)],
        scratch_shapes=[pltpu.VMEM((SUBLANES, dff), F32), pltpu.VMEM((tm, dff), BF16)],
        compiler_params=_cparams(("arbitrary",)),
        name="ffn_prompt",
    )(x, wa, wu, wd, cw, cb, g, b)


def _ffn_sample_kernel(x_ref, s0_ref, s1_ref, wa_ref, wu_ref, wd_ref, cw_ref, cb_ref, g_ref, b_ref,
                       y_ref, a_ref, wab_ref, wub_ref, wdb_ref, acc_ref):
    j = pl.program_id(0)
    x = x_ref[...]
    xb = x.astype(BF16)
    wa, wu, wd = wa_ref[...].astype(BF16), wu_ref[...].astype(BF16), wd_ref[...].astype(BF16)
    wab_ref[...] = wa
    wub_ref[...] = wu
    wdb_ref[...] = wd
    a = _dot(xb, wa)
    u = _dot(xb, wu)
    a_ref[...] = a
    h = _conv_gate(a, s1_ref[...], s0_ref[...], u, cw_ref[...], cb_ref[...])
    dn = _dot(h, wd)

    @pl.when(j == 0)
    def _():
        acc_ref[...] = dn

    @pl.when(j > 0)
    def _():
        acc_ref[...] += dn

    @pl.when(j == pl.num_programs(0) - 1)
    def _():
        y_ref[...] = _layer_norm(ALPHA * x + acc_ref[...], g_ref[...], b_ref[...])


def _ffn_sample(x, s0, s1, layer, wup, wd, cw, cb, g, b):
    n, d = x.shape
    dff = wd.shape[1]
    tf = FFN_TF
    nchunk = dff // tf
    col = pl.BlockSpec((n, tf), lambda j: (0, j))
    return pl.pallas_call(
        _ffn_sample_kernel,
        grid=(nchunk,),
        in_specs=[
            pl.BlockSpec((n, d), lambda j: (0, 0)), col, col,
            pl.BlockSpec((None, d, tf), lambda j: (layer, 0, j)),
            pl.BlockSpec((None, d, tf), lambda j: (layer, 0, j + nchunk)),
            pl.BlockSpec((None, tf, d), lambda j: (layer, j, 0)),
            pl.BlockSpec((None, CONV_W, tf), lambda j: (layer, 0, j)),
            pl.BlockSpec((None, 1, tf), lambda j: (layer, 0, j)),
            pl.BlockSpec((None, 1, d), lambda j: (layer, 0, 0)),
            pl.BlockSpec((None, 1, d), lambda j: (layer, 0, 0)),
        ],
        out_specs=[pl.BlockSpec((n, d), lambda j: (0, 0)), col,
                   pl.BlockSpec((d, tf), lambda j: (0, j)), pl.BlockSpec((d, tf), lambda j: (0, j)),
                   pl.BlockSpec((tf, d), lambda j: (j, 0))],
        out_shape=[jax.ShapeDtypeStruct((n, d), F32), jax.ShapeDtypeStruct((n, dff), F32),
                   jax.ShapeDtypeStruct((d, dff), BF16), jax.ShapeDtypeStruct((d, dff), BF16),
                   jax.ShapeDtypeStruct((dff, d), BF16)],
        scratch_shapes=[pltpu.VMEM((n, d), F32)],
        compiler_params=_cparams(("arbitrary",)),
        name="ffn_sample",
    )(x, s0, s1, wup, wup, wd, cw, cb, g, b)


def _gla_decay_logits(xb, wa1_ref, wa2_ref, ba_ref):
    t1 = _dot(xb, wa1_ref[...])
    z = _dot(t1.astype(BF16), wa2_ref[...]) + ba_ref[...]
    return jax.nn.log_sigmoid(z) * (1.0 / GLA_TAU)


def _gla_out_gate(o, gate, ng):
    outs = []
    for h in range(GLA_HEADS):
        oh = o[:, h * GLA_DVH:(h + 1) * GLA_DVH]
        ms = jnp.mean(oh * oh, axis=-1, keepdims=True)
        outs.append(oh * lax.rsqrt(ms + NORM_EPS) * ng)
    on = jnp.concatenate(outs, axis=1)
    return (on * jax.nn.silu(gate)).astype(BF16)


def _split3(t):
    hi = t.astype(BF16)
    r1 = t - hi.astype(F32)
    mid = r1.astype(BF16)
    lo = (r1 - mid.astype(F32)).astype(BF16)
    return hi, mid, lo


def _gla_prompt_kernel(x_ref, win_ref, wa1_ref, wa2_ref, ba_ref, ng_ref, wo_ref, g_ref, b_ref,
                       y_ref, so_ref, s_ref, proj_ref, u_ref, o_ref):
    ti = pl.program_id(1)
    ck = GLA_CHUNK

    @pl.when(ti == 0)
    def _():
        s_ref[...] = jnp.zeros_like(s_ref)

    x = x_ref[0]
    xb = x.astype(BF16)
    tm = x.shape[0]
    nck = tm // ck
    la = _gla_decay_logits(xb, wa1_ref, wa2_ref, ba_ref)
    c_v, c_g = 2 * GLA_DK, 2 * GLA_DK + GLA_DV
    proj_ref[:, :c_v] = _dot(xb, win_ref[:, :c_v])
    la_parts = _split3(la)

    r = lax.broadcasted_iota(jnp.int32, (ck, ck), 0)
    c = lax.broadcasted_iota(jnp.int32, (ck, ck), 1)
    causal = r >= c
    tril = causal.astype(BF16)

    tril3 = jnp.concatenate([tril] * 3 + [jnp.zeros_like(tril)], axis=1)
    zrows = jnp.zeros((ck, GLA_DK), BF16)
    bcs = [_dot(tril3, jnp.concatenate([part[ci * ck:(ci + 1) * ck] for part in la_parts] + [zrows],
                                       axis=0))
           for ci in range(nck)]

    proj_ref[:, c_v:c_g] = _dot(xb, win_ref[:, c_v:c_g])
    q_in, vb, att, dec = {}, {}, {}, {}
    for ci in range(nck):
        rows = slice(ci * ck, (ci + 1) * ck)
        bc = bcs[ci]
        for h in range(GLA_HEADS):
            bh = bc[:, h * GLA_DKH:(h + 1) * GLA_DKH]
            q = proj_ref[rows, h * GLA_DKH:(h + 1) * GLA_DKH] * (GLA_DKH ** -0.5)
            k = proj_ref[rows, GLA_DK + h * GLA_DKH:GLA_DK + (h + 1) * GLA_DKH]
            v = proj_ref[rows, 2 * GLA_DK + h * GLA_DVH:2 * GLA_DK + (h + 1) * GLA_DVH].astype(BF16)
            qi = (q * jnp.exp(bh)).astype(BF16)
            k_in = (k * jnp.exp(-bh)).astype(BF16)
            bl = bh[ck - 1:ck]
            kd = (k * jnp.exp(bl - bh)).astype(BF16)
            att[ci, h] = _dot_nt(qi, k_in)
            u_ref[ci, h] = _dot_tn(kd, v)
            q_in[ci, h], vb[ci, h], dec[ci, h] = qi, v, jnp.exp(bl)

    proj_ref[:, c_g:] = _dot(xb, win_ref[:, c_g:])
    state = [s_ref[h] for h in range(GLA_HEADS)]
    zpad_l = jnp.zeros((ck, ck), BF16)
    zpad_r = jnp.zeros((ck, GLA_DVH), BF16)
    for ci in range(nck):
        rows = slice(ci * ck, (ci + 1) * ck)
        for h in range(GLA_HEADS):
            am = jnp.where(causal, att[ci, h], 0.0).astype(BF16)
            lhs = jnp.concatenate([q_in[ci, h], am, zpad_l], axis=1)
            rhs = jnp.concatenate([state[h].astype(BF16), vb[ci, h], zpad_r], axis=0)
            o_ref[rows, h * GLA_DVH:(h + 1) * GLA_DVH] = _dot(lhs, rhs)
            dcol = jnp.transpose(jnp.broadcast_to(dec[ci, h], (GLA_DKH, GLA_DKH)))
            dfull = jnp.concatenate([dcol] * (GLA_DVH // GLA_DKH), axis=1)
            state[h] = dfull * state[h] + u_ref[ci, h]
    for h in range(GLA_HEADS):
        s_ref[h] = state[h]

    rt = tm // GLA_ROW_SPLITS
    for rs in range(GLA_ROW_SPLITS):
        rows = slice(rs * rt, (rs + 1) * rt)
        og = _gla_out_gate(o_ref[rows, :], proj_ref[rows, 2 * GLA_DK + GLA_DV:], ng_ref[...])
        z = ALPHA * x[rows] + _dot(og, wo_ref[...])
        y_ref[0, rows, :] = _layer_norm(z, g_ref[...], b_ref[...])

    @pl.when(ti == pl.num_programs(1) - 1)
    def _():
        so_ref[0] = s_ref[...]


def _gla_prompt(x, win, wa1, wa2, ba, ng, wo, g, b):
    bsz, seq, d = x.shape
    tm = min(GLA_TM, seq)
    nin = win.shape[1]
    st = (GLA_HEADS, GLA_DKH, GLA_DVH)
    return pl.pallas_call(
        _gla_prompt_kernel,
        grid=(bsz, seq // tm),
        in_specs=[
            pl.BlockSpec((1, tm, d), lambda bi, ti: (bi, ti, 0)),
            _resident(win.shape), _resident(wa1.shape), _resident(wa2.shape), _resident(ba.shape),
            _resident(ng.shape), _resident(wo.shape), _resident(g.shape), _resident(b.shape),
        ],
        out_specs=[pl.BlockSpec((1, tm, d), lambda bi, ti: (bi, ti, 0)),
                   pl.BlockSpec((1,) + st, lambda bi, ti: (bi, 0, 0, 0))],
        out_shape=[jax.ShapeDtypeStruct((bsz, seq, d), F32),
                   jax.ShapeDtypeStruct((bsz,) + st, F32)],
        scratch_shapes=[pltpu.VMEM(st, F32), pltpu.VMEM((tm, nin), F32),
                        pltpu.VMEM((tm // GLA_CHUNK,) + st, F32), pltpu.VMEM((tm, GLA_DV), F32)],
        compiler_params=_cparams(("arbitrary", "arbitrary")),
        name="gla_prompt",
    )(x, win, wa1, wa2, ba, ng, wo, g, b)


def _linear_kernel(x_ref, w_ref, b_ref, o_ref):
    o_ref[...] = _dot(x_ref[...].astype(BF16), w_ref[...]) + b_ref[...]


def _linear_cast_kernel(x_ref, w_ref, b_ref, o_ref, wb_ref):
    wb = w_ref[...].astype(BF16)
    wb_ref[...] = wb
    o_ref[...] = _dot(x_ref[...].astype(BF16), wb) + b_ref[...]


def _linear(x, w, bias):
    n, kdim = x.shape
    nout = w.shape[1]
    tn = next(t for t in (512, 256, 128) if nout % t == 0)
    emit = w.dtype != BF16
    wspec = pl.BlockSpec((kdim, tn), lambda j: (0, j))
    out_specs = [pl.BlockSpec((n, tn), lambda j: (0, j))]
    out_shape = [jax.ShapeDtypeStruct((n, nout), F32)]
    if emit:
        out_specs.append(wspec)
        out_shape.append(jax.ShapeDtypeStruct(w.shape, BF16))
    outs = pl.pallas_call(
        _linear_cast_kernel if emit else _linear_kernel,
        grid=(nout // tn,),
        in_specs=[pl.BlockSpec((n, kdim), lambda j: (0, 0)), wspec,
                  pl.BlockSpec((1, tn), lambda j: (0, j))],
        out_specs=out_specs,
        out_shape=out_shape,
        compiler_params=_cparams(("arbitrary",)),
        name="sample_linear",
    )(x, w, bias)
    return tuple(outs) if emit else outs[0]


def _linear_ln_kernel(h_ref, w_ref, r_ref, g_ref, b_ref, o_ref, wb_ref):
    wb = w_ref[...].astype(BF16)
    wb_ref[...] = wb
    z = ALPHA * r_ref[...] + _dot(h_ref[...].astype(BF16), wb)
    o_ref[...] = _layer_norm(z, g_ref[...], b_ref[...])


def _linear_ln(h, w, resid, g, b):
    n, d = resid.shape
    full = lambda a: pl.BlockSpec(a.shape, lambda i: (0,) * a.ndim)
    return pl.pallas_call(
        _linear_ln_kernel,
        grid=(1,),
        in_specs=[full(h), full(w), full(resid), full(g), full(b)],
        out_specs=[pl.BlockSpec((n, d), lambda i: (0, 0)), full(w)],
        out_shape=[jax.ShapeDtypeStruct((n, d), F32), jax.ShapeDtypeStruct(w.shape, BF16)],
        compiler_params=_cparams(("arbitrary",)),
        name="sample_linear_ln",
    )(h, w, resid, g, b)


def _attn_sample_kernel(q2_ref, kv_ref, ck_ref, cv_ref, cos_ref, sa_ref, sb_ref, sink_ref,
                        o2_ref, kn_ref, vn_ref):
    nseq, wbuf, _ = ck_ref.shape
    cos, sa, sb = cos_ref[...], sa_ref[...], sb_ref[...]
    q2 = (_rope_slab(q2_ref[...], cos, sa, sb) * ATTN_SCALE).astype(BF16)
    kv = kv_ref[...]
    k_all = _rope_slab(kv[:, :LANES], cos, sa, sb)
    v_all = kv[:, LANES:]
    last = lax.broadcasted_iota(jnp.int32, (wbuf, LANES), 0) == wbuf - 1
    sink = sink_ref[...]

    k_new, v_new = [], []
    for s in range(nseq):
        k_new.append(jnp.where(last, k_all[s:s + 1], pltpu.roll(ck_ref[s], wbuf - 1, 0)))
        v_new.append(jnp.where(last, v_all[s:s + 1], pltpu.roll(cv_ref[s], wbuf - 1, 0)))
        kn_ref[s] = k_new[s]
        vn_ref[s] = v_new[s]
    sc = [_dot_nt(q2[s * N_HEADS:(s + 1) * N_HEADS], k_new[s].astype(BF16)) for s in range(nseq)]
    ps = [_sink_softmax(sc[s], sink).astype(BF16) for s in range(nseq)]
    for s in range(nseq):
        o2_ref[s * N_HEADS:(s + 1) * N_HEADS, :] = _dot(ps[s], v_new[s].astype(BF16))


def _attn_sample(qkv, ck, cv, sinks):
    n = qkv.shape[0]
    wbuf = ck.shape[1]
    sb_ = SAMPLE_SEQ_BLOCK
    nq = N_HEADS * HEAD_DIM
    cos, sa, sb = _rope_tables(jnp.full((1,), PAST_LEN, dtype=F32))
    one = lambda a: pl.BlockSpec(a.shape, lambda i: (0,) * a.ndim)
    sink_col = sinks.reshape(N_HEADS, 1)
    q = qkv[:, :nq].reshape(n, N_HEADS, HEAD_DIM)
    zero = jnp.zeros_like(q)
    first_group = (jnp.arange(N_HEADS) < GROUP)[None, :, None]
    q2 = jnp.concatenate([jnp.where(first_group, q, zero), jnp.where(first_group, zero, q)], axis=-1)
    q2 = q2.reshape(n * N_HEADS, LANES)
    cache = pl.BlockSpec((sb_, wbuf, LANES), lambda i: (i, 0, 0))
    rows16 = pl.BlockSpec((sb_ * N_HEADS, LANES), lambda i: (i, 0))
    o2, k_s, v_s = pl.pallas_call(
        _attn_sample_kernel,
        grid=(n // sb_,),
        in_specs=[rows16, pl.BlockSpec((sb_, 2 * LANES), lambda i: (i, 0)), cache, cache,
                  one(cos), one(sa), one(sb), one(sink_col)],
        out_specs=[rows16, cache, cache],
        out_shape=[jax.ShapeDtypeStruct((n * N_HEADS, LANES), F32),
                   jax.ShapeDtypeStruct(ck.shape, F32), jax.ShapeDtypeStruct(cv.shape, F32)],
        compiler_params=_cparams(("arbitrary",)),
        name="attn_sample",
    )(q2, qkv[:, nq:], ck, cv, cos, sa, sb, sink_col)
    o2 = o2.reshape(n, N_HEADS, N_KV_HEADS, HEAD_DIM)
    o = jnp.where(first_group, o2[:, :, 0, :], o2[:, :, 1, :])
    return o.reshape(n, nq), k_s, v_s


def _gla_sample_kernel(proj_ref, z_ref, s0_ref, ng_ref, o_ref, sn_ref, oacc_ref):
    proj = proj_ref[...]
    la = jax.nn.log_sigmoid(z_ref[...]) * (1.0 / GLA_TAU)
    nseq = proj.shape[0]
    pad = jnp.zeros((LANES - nseq, GLA_DKH), F32)

    def cols(t):
        return jnp.transpose(jnp.concatenate([t, pad], axis=0))

    for h in range(GLA_HEADS):
        b = la[:, h * GLA_DKH:(h + 1) * GLA_DKH]
        q = proj[:, h * GLA_DKH:(h + 1) * GLA_DKH] * (GLA_DKH ** -0.5)
        k = proj[:, GLA_DK + h * GLA_DKH:GLA_DK + (h + 1) * GLA_DKH]
        v = proj[:, 2 * GLA_DK + h * GLA_DVH:2 * GLA_DK + (h + 1) * GLA_DVH]
        eb = jnp.exp(b)
        q_in = q * eb
        k_in = k * jnp.exp(-b)
        att = jnp.sum(q_in * k_in, axis=-1, keepdims=True)
        kd = k * jnp.exp(b - b)
        qc, kc, ec = cols(q_in), cols(kd), cols(eb)
        for s in range(nseq):
            s_old = s0_ref[s, h]
            vrow = v[s:s + 1]
            o_row = att[s:s + 1] * vrow + jnp.sum(qc[:, s:s + 1] * s_old, axis=0, keepdims=True)
            oacc_ref[s:s + 1, h * GLA_DVH:(h + 1) * GLA_DVH] = o_row
            sn_ref[s, h] = ec[:, s:s + 1] * s_old + kc[:, s:s + 1] * vrow
    og = _gla_out_gate(oacc_ref[...], proj[:, 2 * GLA_DK + GLA_DV:], ng_ref[...])
    o_ref[...] = og.astype(F32)


def _gla_sample(proj, z, s0, ng):
    n = proj.shape[0]
    sb_ = SAMPLE_SEQ_BLOCK
    st = pl.BlockSpec((sb_,) + s0.shape[1:], lambda i: (i, 0, 0, 0))
    return pl.pallas_call(
        _gla_sample_kernel,
        grid=(n // sb_,),
        in_specs=[pl.BlockSpec((sb_, proj.shape[1]), lambda i: (i, 0)),
                  pl.BlockSpec((sb_, z.shape[1]), lambda i: (i, 0)), st,
                  pl.BlockSpec(ng.shape, lambda i: (0, 0))],
        out_specs=[pl.BlockSpec((sb_, GLA_DV), lambda i: (i, 0)), st],
        out_shape=[jax.ShapeDtypeStruct((n, GLA_DV), F32), jax.ShapeDtypeStruct(s0.shape, F32)],
        scratch_shapes=[pltpu.VMEM((sb_, GLA_DV), F32)],
        compiler_params=_cparams(("arbitrary",)),
        name="gla_sample",
    )(proj, z, s0, ng)


def _row(v):
    return v.reshape(1, -1)


def kernel(x_prompt, x_sample, cache_k, cache_v, state_gla, state_conv, attn_w_qkv, attn_b_qkv, attn_sinks, attn_w_o, gla_w_in, gla_w_a1, gla_w_a2, gla_b_a, gla_norm_g, gla_w_o, ffn_w_up, ffn_conv_w, ffn_conv_b, ffn_w_down, ln_mix_g, ln_mix_b, ln_ffn_g, ln_ffn_b):
    bsz, seq, d = x_prompt.shape
    nsmp = x_sample.shape[0]
    xp = x_prompt
    xs = x_sample.reshape(nsmp, d)
    zeros = lambda n: jnp.zeros((1, n), F32)
    ffn_small = (ffn_conv_w, ffn_conv_b[:, None, :], ln_ffn_g[:, None, :], ln_ffn_b[:, None, :])

    kp_l, vp_l, ks_l, vs_l, sp_l, ss_l, cp_l, cs_l = [], [], [], [], [], [], [], []
    for i in range(DEPTH):
        j = i // 2
        g_mix, b_mix = _row(ln_mix_g[i]), _row(ln_mix_b[i])
        if i % 2 == 0:
            bqkv = _row(attn_b_qkv[j])
            wbuf = cache_k.shape[2]
            qkv_s, wqkv = _linear(xs, attn_w_qkv[j], bqkv)
            o_s, k_s, v_s = _attn_sample(qkv_s, cache_k[j].reshape(nsmp, wbuf, LANES),
                                         cache_v[j].reshape(nsmp, wbuf, LANES), attn_sinks[j])
            xs, wo = _linear_ln(o_s, attn_w_o[j], xs, g_mix, b_mix)
            ks_l.append(k_s.reshape(cache_k.shape[1:]))
            vs_l.append(v_s.reshape(cache_v.shape[1:]))

            xp, kp, vp = _attn_prompt(xp, wqkv, bqkv, attn_sinks[j], wo, g_mix, b_mix)
            kv_shape = (bsz, WINDOW, N_KV_HEADS, HEAD_DIM)
            kp_l.append(kp.reshape(kv_shape))
            vp_l.append(vp.reshape(kv_shape))
        else:
            wa1 = jnp.pad(gla_w_a1[j], ((0, 0), (0, LANES - GLA_RANK))).astype(BF16)
            wa2 = jnp.pad(gla_w_a2[j], ((0, LANES - GLA_RANK), (0, 0))).astype(BF16)
            ba = _row(gla_b_a[j])
            ng = _row(gla_norm_g[j])
            proj_s, win = _linear(xs, gla_w_in[j], zeros(gla_w_in.shape[2]))
            t1 = _linear(xs, wa1, zeros(LANES))
            z_s = _linear(t1, wa2, ba)
            o_s, s_s = _gla_sample(proj_s, z_s, state_gla[j], ng)
            xs, wo = _linear_ln(o_s, gla_w_o[j], xs, g_mix, b_mix)
            ss_l.append(s_s)

            xp, sp = _gla_prompt(xp, win, wa1, wa2, ba, ng, wo, g_mix, b_mix)
            sp_l.append(sp)

        xs, a_s, wa, wu, wd = _ffn_sample(xs, state_conv[i, :, 0], state_conv[i, :, 1], i,
                                          ffn_w_up, ffn_w_down, *ffn_small)
        cs_l.append(jnp.stack([state_conv[i, :, 1], a_s], axis=1))
        xp2, cp = _ffn_prompt(xp.reshape(bsz * seq, d), seq, i, wa, wu, wd, *ffn_small)
        xp = xp2.reshape(bsz, seq, d)
        cp_l.append(cp)

    return (xp, xs.reshape(x_sample.shape), jnp.stack(kp_l), jnp.stack(vp_l), jnp.stack(ks_l),
            jnp.stack(vs_l), jnp.stack(sp_l), jnp.stack(ss_l), jnp.stack(cp_l), jnp.stack(cs_l))
```

```python
import functools

import jax
import jax.numpy as jnp
from jax import lax
from jax.experimental import pallas as pl
from jax.experimental.pallas import tpu as pltpu

F32 = jnp.float32
BF16 = jnp.bfloat16

D_MODEL = 1024
DEPTH = 2
PAST_LEN = 16384
N_HEADS = 16
N_KV_HEADS = 2
HEAD_DIM = 64
GROUP = N_HEADS // N_KV_HEADS
WINDOW = 128
ROPE_THETA = 10000.0
ATTN_SCALE = HEAD_DIM ** -0.5
GLA_HEADS = 4
GLA_DK = D_MODEL // 2
GLA_DV = D_MODEL
GLA_DKH = GLA_DK // GLA_HEADS
GLA_DVH = GLA_DV // GLA_HEADS
GLA_RANK = 16
GLA_TAU = 16.0
GLA_CHUNK = 64
D_FF = 2816
CONV_W = 3
ALPHA = (2 * DEPTH) ** 0.25
LN_EPS = 1e-5
NORM_EPS = 1e-6
NEG_INF = -1e30

LANES = 128
SUBLANES = 8
VMEM_LIMIT_BYTES = 56 * 1024 * 1024

ATTN_TQ = 2048
FFN_TM = 1024
FFN_TF = 256
FFN_ROW_SPLITS = 4
GLA_TM = 1024
GLA_ROW_SPLITS = 4
SAMPLE_SEQ_BLOCK = 8


def _cparams(sem):
    return pltpu.CompilerParams(dimension_semantics=sem, vmem_limit_bytes=VMEM_LIMIT_BYTES)


def _resident(shape):
    nd = len(shape)
    return pl.BlockSpec(shape, lambda *_: (0,) * nd, pipeline_mode=pl.Buffered(1))


def _resident_layer(stacked, layer):
    nd = stacked.ndim - 1
    return pl.BlockSpec((None,) + stacked.shape[1:], lambda *_: (layer,) + (0,) * nd,
                        pipeline_mode=pl.Buffered(1))


def _layer_norm(z, g, b):
    mu = jnp.mean(z, axis=-1, keepdims=True)
    zc = z - mu
    var = jnp.mean(zc * zc, axis=-1, keepdims=True)
    return zc * lax.rsqrt(var + LN_EPS) * g + b


def _gelu(c):
    return 0.5 * c * (1.0 + lax.erf(c * (2.0 ** -0.5)))


def _dot(a, b):
    return jnp.dot(a, b, preferred_element_type=F32)


def _dot_nt(a, b):
    return lax.dot_general(a, b, (((1,), (1,)), ((), ())), preferred_element_type=F32)


def _dot_tn(a, b):
    return lax.dot_general(a, b, (((0,), (0,)), ((), ())), preferred_element_type=F32)


def _rope_tables(pos):
    inv = 1.0 / (ROPE_THETA ** (jnp.arange(0, HEAD_DIM, 2, dtype=F32) / HEAD_DIM))
    ang = pos[:, None] * inv[None, :]
    cos, sin = jnp.cos(ang), jnp.sin(ang)
    zero = jnp.zeros_like(sin)
    cos_h = jnp.concatenate([cos, cos], -1)
    sa_h = jnp.concatenate([-sin, zero], -1)
    sb_h = jnp.concatenate([zero, sin], -1)
    two = lambda t: jnp.concatenate([t, t], -1)
    return two(cos_h), two(sa_h), two(sb_h)


def _rope_slab(s, cos, sa, sb):
    return (s * cos + pltpu.roll(s, LANES - HEAD_DIM // 2, 1) * sa
            + pltpu.roll(s, HEAD_DIM // 2, 1) * sb)


HALF = HEAD_DIM // 2


def _paired_perm():
    base = jnp.arange(HALF)
    return jnp.concatenate([base, HEAD_DIM + base, HALF + base, HEAD_DIM + HALF + base])


def _rope_tables_paired(pos):
    inv = 1.0 / (ROPE_THETA ** (jnp.arange(0, HEAD_DIM, 2, dtype=F32) / HEAD_DIM))
    ang = pos[:, None] * inv[None, :]
    cos, sin = jnp.cos(ang), jnp.sin(ang)
    return (jnp.concatenate([cos] * 4, -1), jnp.concatenate([-sin, -sin, sin, sin], -1))


def _rope_paired(s, cos, sin_signed):
    return s * cos + pltpu.roll(s, HEAD_DIM, 1) * sin_signed


def _paired_split(t):
    lane = lax.broadcasted_iota(jnp.int32, t.shape, 1)
    on_a = (lane % HEAD_DIM) < HALF
    z = jnp.zeros_like(t)
    return (jnp.where(on_a, t, z), jnp.where(on_a, z, pltpu.roll(t, HALF, 1)),
            jnp.where(on_a, pltpu.roll(t, LANES - HALF, 1), z), jnp.where(on_a, z, t))


def _paired_to_standard(t):
    lane = lax.broadcasted_iota(jnp.int32, t.shape, 1)
    keep = (lane < HALF) | (lane >= LANES - HALF)
    return jnp.where(keep, t, jnp.where(lane < HEAD_DIM, pltpu.roll(t, LANES - HALF, 1),
                                        pltpu.roll(t, HALF, 1)))


def _sink_softmax(s, sink):
    m = jnp.maximum(jnp.max(s, axis=-1, keepdims=True), sink)
    p = jnp.exp(s - m)
    den = jnp.sum(p, axis=-1, keepdims=True) + jnp.exp(sink - m)
    return p * (1.0 / den)


def _lane_split_pair(t):
    lo = lax.broadcasted_iota(jnp.int32, t.shape, 1) < HEAD_DIM
    tr = pltpu.roll(t, HEAD_DIM, 1)
    z = jnp.zeros_like(t)
    return (jnp.where(lo, t, z), jnp.where(lo, z, tr), jnp.where(lo, tr, z), jnp.where(lo, z, t))


def _attn_prompt_kernel(sinks_ref, x_ref, cos_ref, sin_ref, wqkv_ref, bqkv_ref, wo_ref,
                        g_ref, b_ref, y_ref, kn_ref, vn_ref, kprev_ref, vprev_ref):
    ti = pl.program_id(1)
    nq = N_HEADS * HEAD_DIM
    w = WINDOW

    @pl.when(ti == 0)
    def _():
        kprev_ref[...] = jnp.zeros_like(kprev_ref)
        vprev_ref[...] = jnp.zeros_like(vprev_ref)

    nblk = x_ref.shape[1] // w
    nslab = nq // LANES
    spg = nslab // N_KV_HEADS

    t = lax.broadcasted_iota(jnp.int32, (w, w), 0)
    s = lax.broadcasted_iota(jnp.int32, (w, w), 1)
    upper = s > t
    has_prev = ti > 0

    def project(blk):
        rows = slice(blk * w, (blk + 1) * w)
        x = x_ref[0, rows, :]
        qkv = _dot(x.astype(BF16), wqkv_ref[...]) + bqkv_ref[...]
        cos, sin = cos_ref[rows, :], sin_ref[rows, :]
        k = _rope_paired(qkv[:, nq:nq + LANES], cos, sin)
        v = qkv[:, nq + LANES:nq + 2 * LANES]
        q = [(_rope_paired(qkv[:, j * LANES:(j + 1) * LANES], cos, sin) * ATTN_SCALE).astype(BF16)
             for j in range(nslab)]
        return dict(x=x, q=q, k=k, v=v, ks=_paired_split(k.astype(BF16)),
                    vs=_lane_split_pair(v.astype(BF16)))

    def finish(blk, st, pv):
        slabs = []
        for g in range(N_KV_HEADS):
            og = pv[g, 0] + pv[g, 1]
            slabs += [og[i * w:(i + 1) * w] for i in range(spg)]
        o = jnp.concatenate(slabs, axis=1).astype(BF16)
        z = ALPHA * st["x"] + _dot(o, wo_ref[...])
        y_ref[0, blk * w:(blk + 1) * w, :] = _layer_norm(z, g_ref[...], b_ref[...])

    units = [(g, par) for g in range(N_KV_HEADS) for par in range(2)]
    prev = dict(ks=_paired_split(kprev_ref[...].astype(BF16)),
                vs=_lane_split_pair(vprev_ref[...].astype(BF16)))
    stages = {-1: prev, 0: project(0)}
    pvs = {}
    for blk in range(nblk):
        if blk + 1 < nblk:
            stages[blk + 1] = project(blk + 1)
        st, pst = stages[blk], stages[blk - 1]

        def scores(u):
            g, par = u
            qg = jnp.concatenate([st["q"][g * spg + i] for i in range(spg)], axis=0)
            kmat = jnp.concatenate([pst["ks"][2 * g + par], st["ks"][2 * g + par]], axis=0)
            return _dot_nt(qg, kmat)

        pv = {}
        sc_nxt = scores(units[0])
        for n, (g, par) in enumerate(units):
            sc = sc_nxt
            if n + 1 < len(units):
                sc_nxt = scores(units[n + 1])
            ps = []
            for i in range(spg):
                head = 2 * (g * spg + i) + par
                s_prev = sc[i * w:(i + 1) * w, :w]
                if blk == 0:
                    s_prev = jnp.where(has_prev, s_prev, NEG_INF)
                p = _sink_softmax(jnp.where(upper, s_prev, sc[i * w:(i + 1) * w, w:]), sinks_ref[head])
                zero = jnp.zeros_like(p)
                ps.append(jnp.concatenate([jnp.where(upper, p, zero), jnp.where(upper, zero, p)],
                                          axis=1).astype(BF16))
            vmat = jnp.concatenate([pst["vs"][2 * g + par], st["vs"][2 * g + par]], axis=0)
            pv[g, par] = _dot(jnp.concatenate(ps, axis=0), vmat)
        pvs[blk] = pv
        if blk >= 1:
            finish(blk - 1, stages[blk - 1], pvs[blk - 1])
    finish(nblk - 1, stages[nblk - 1], pvs[nblk - 1])

    k_last, v_last = stages[nblk - 1]["k"], stages[nblk - 1]["v"]
    kprev_ref[...] = k_last
    vprev_ref[...] = v_last

    @pl.when(ti == pl.num_programs(1) - 1)
    def _():
        kn_ref[0] = _paired_to_standard(k_last)
        vn_ref[0] = v_last


def _attn_prompt(x, wqkv, bqkv, sinks, wo, g, b):
    bsz, seq, d = x.shape
    tq = min(ATTN_TQ, seq)
    nqkv = wqkv.shape[1]
    nq = N_HEADS * HEAD_DIM
    perm = _paired_perm()
    cols = jnp.concatenate([s * LANES + perm for s in range((nq + LANES) // LANES)]
                           + [jnp.arange(nq + LANES, nqkv)])
    wqkv, bqkv = wqkv[:, cols], bqkv[:, cols]
    cos, sin = _rope_tables_paired(jnp.arange(seq, dtype=F32))
    tab = pl.BlockSpec((tq, LANES), lambda bi, ti: (ti, 0))
    kv_out = pl.BlockSpec((1, WINDOW, LANES), lambda bi, ti: (bi, 0, 0))
    return pl.pallas_call(
        _attn_prompt_kernel,
        grid=(bsz, seq // tq),
        in_specs=[
            pl.BlockSpec(memory_space=pltpu.SMEM),
            pl.BlockSpec((1, tq, d), lambda bi, ti: (bi, ti, 0)),
            tab, tab,
            _resident((d, nqkv)), _resident((1, nqkv)), _resident((d, d)),
            _resident((1, d)), _resident((1, d)),
        ],
        out_specs=[pl.BlockSpec((1, tq, d), lambda bi, ti: (bi, ti, 0)), kv_out, kv_out],
        out_shape=[jax.ShapeDtypeStruct((bsz, seq, d), F32),
                   jax.ShapeDtypeStruct((bsz, WINDOW, LANES), F32),
                   jax.ShapeDtypeStruct((bsz, WINDOW, LANES), F32)],
        scratch_shapes=[pltpu.VMEM((WINDOW, LANES), F32), pltpu.VMEM((WINDOW, LANES), F32)],
        compiler_params=_cparams(("arbitrary", "arbitrary")),
        name="attn_prompt",
    )(sinks, x, cos, sin, wqkv, bqkv, wo, g, b)


def _conv_gate(a, a1, a2, u, cw, cb):
    c = cb + cw[0:1] * a2 + cw[1:2] * a1 + cw[2:3] * a
    return (_gelu(c) * u).astype(BF16)


def _ffn_prompt_kernel(x_ref, wa_ref, wu_ref, wd_ref, cw_ref, cb_ref, g_ref, b_ref,
                       y_ref, cs_ref, carry_ref, h_ref, *, tiles_per_seq):
    i = pl.program_id(0)
    pos = i % tiles_per_seq

    @pl.when(pos == 0)
    def _():
        carry_ref[...] = jnp.zeros_like(carry_ref)

    x = x_ref[...]
    xb = x.astype(BF16)
    tm = x.shape[0]
    dff = wd_ref.shape[0]
    tf = FFN_TF
    row = lax.broadcasted_iota(jnp.int32, (SUBLANES, tf), 0)
    for j in range(dff // tf):
        sl = slice(j * tf, (j + 1) * tf)
        a = _dot(xb, wa_ref[:, sl])
        u = _dot(xb, wu_ref[:, sl])
        prev = carry_ref[:, sl]
        p1 = prev[SUBLANES - 1:SUBLANES]
        p2 = prev[SUBLANES - 2:SUBLANES - 1]
        a1 = pltpu.roll(a, 1, 0)
        a2 = pltpu.roll(a, 2, 0)
        a1_top = jnp.where(row == 0, p1, a1[:SUBLANES])
        a2_top = jnp.where(row == 0, p2, jnp.where(row == 1, p1, a2[:SUBLANES]))
        a1 = jnp.concatenate([a1_top, a1[SUBLANES:]], axis=0)
        a2 = jnp.concatenate([a2_top, a2[SUBLANES:]], axis=0)
        carry_ref[:, sl] = a[tm - SUBLANES:tm]
        h_ref[:, sl] = _conv_gate(a, a1, a2, u, cw_ref[:, sl], cb_ref[:, sl])

    rt = tm // FFN_ROW_SPLITS
    for r in range(FFN_ROW_SPLITS):
        rows = slice(r * rt, (r + 1) * rt)
        z = ALPHA * x[rows] + _dot(h_ref[rows, :], wd_ref[...])
        y_ref[rows, :] = _layer_norm(z, g_ref[...], b_ref[...])

    @pl.when(pos == tiles_per_seq - 1)
    def _():
        cs_ref[0] = carry_ref[SUBLANES - (CONV_W - 1):SUBLANES, :]


def _ffn_prompt(x, seq, layer, wa, wu, wd, cw, cb, g, b):
    n, d = x.shape
    tm = min(FFN_TM, seq)
    tiles_per_seq = seq // tm
    dff = wd.shape[0]
    return pl.pallas_call(
        functools.partial(_ffn_prompt_kernel, tiles_per_seq=tiles_per_seq),
        grid=(n // tm,),
        in_specs=[pl.BlockSpec((tm, d), lambda i: (i, 0))]
        + [_resident(a.shape) for a in (wa, wu, wd)]
        + [_resident_layer(a, layer) for a in (cw, cb, g, b)],
        out_specs=[pl.BlockSpec((tm, d), lambda i: (i, 0)),
                   pl.BlockSpec((1, CONV_W - 1, dff), lambda i: (i // tiles_per_seq, 0, 0))],
        out_shape=[jax.ShapeDtypeStruct((n, d), F32),
                   jax.ShapeDtypeStruct((n // seq, CONV_W - 1, dff), F32)],
        scratch_shapes=[pltpu.VMEM((SUBLANES, dff), F32), pltpu.VMEM((tm, dff), BF16)],
        compiler_params=_cparams(("arbitrary",)),
        name="ffn_prompt",
    )(x, wa, wu, wd, cw, cb, g, b)


def _ffn_sample_kernel(x_ref, s0_ref, s1_ref, wa_ref, wu_ref, wd_ref, cw_ref, cb_ref, g_ref, b_ref,
                       y_ref, a_ref, wab_ref, wub_ref, wdb_ref, acc_ref):
    j = pl.program_id(0)
    x = x_ref[...]
    xb = x.astype(BF16)
    wa, wu, wd = wa_ref[...].astype(BF16), wu_ref[...].astype(BF16), wd_ref[...].astype(BF16)
    wab_ref[...] = wa
    wub_ref[...] = wu
    wdb_ref[...] = wd
    a = _dot(xb, wa)
    u = _dot(xb, wu)
    a_ref[...] = a
    h = _conv_gate(a, s1_ref[...], s0_ref[...], u, cw_ref[...], cb_ref[...])
    dn = _dot(h, wd)

    @pl.when(j == 0)
    def _():
        acc_ref[...] = dn

    @pl.when(j > 0)
    def _():
        acc_ref[...] += dn

    @pl.when(j == pl.num_programs(0) - 1)
    def _():
        y_ref[...] = _layer_norm(ALPHA * x + acc_ref[...], g_ref[...], b_ref[...])


def _ffn_sample(x, s0, s1, layer, wup, wd, cw, cb, g, b):
    n, d = x.shape
    dff = wd.shape[1]
    tf = FFN_TF
    nchunk = dff // tf
    col = pl.BlockSpec((n, tf), lambda j: (0, j))
    return pl.pallas_call(
        _ffn_sample_kernel,
        grid=(nchunk,),
        in_specs=[
            pl.BlockSpec((n, d), lambda j: (0, 0)), col, col,
            pl.BlockSpec((None, d, tf), lambda j: (layer, 0, j)),
            pl.BlockSpec((None, d, tf), lambda j: (layer, 0, j + nchunk)),
            pl.BlockSpec((None, tf, d), lambda j: (layer, j, 0)),
            pl.BlockSpec((None, CONV_W, tf), lambda j: (layer, 0, j)),
            pl.BlockSpec((None, 1, tf), lambda j: (layer, 0, j)),
            pl.BlockSpec((None, 1, d), lambda j: (layer, 0, 0)),
            pl.BlockSpec((None, 1, d), lambda j: (layer, 0, 0)),
        ],
        out_specs=[pl.BlockSpec((n, d), lambda j: (0, 0)), col,
                   pl.BlockSpec((d, tf), lambda j: (0, j)), pl.BlockSpec((d, tf), lambda j: (0, j)),
                   pl.BlockSpec((tf, d), lambda j: (j, 0))],
        out_shape=[jax.ShapeDtypeStruct((n, d), F32), jax.ShapeDtypeStruct((n, dff), F32),
                   jax.ShapeDtypeStruct((d, dff), BF16), jax.ShapeDtypeStruct((d, dff), BF16),
                   jax.ShapeDtypeStruct((dff, d), BF16)],
        scratch_shapes=[pltpu.VMEM((n, d), F32)],
        compiler_params=_cparams(("arbitrary",)),
        name="ffn_sample",
    )(x, s0, s1, wup, wup, wd, cw, cb, g, b)


def _gla_decay_logits(xb, wa1_ref, wa2_ref, ba_ref):
    t1 = _dot(xb, wa1_ref[...])
    z = _dot(t1.astype(BF16), wa2_ref[...]) + ba_ref[...]
    return jax.nn.log_sigmoid(z) * (1.0 / GLA_TAU)


def _gla_out_gate(o, gate, ng):
    outs = []
    for h in range(GLA_HEADS):
        oh = o[:, h * GLA_DVH:(h + 1) * GLA_DVH]
        ms = jnp.mean(oh * oh, axis=-1, keepdims=True)
        outs.append(oh * lax.rsqrt(ms + NORM_EPS) * ng)
    on = jnp.concatenate(outs, axis=1)
    return (on * jax.nn.silu(gate)).astype(BF16)


def _split3(t):
    hi = t.astype(BF16)
    r1 = t - hi.astype(F32)
    mid = r1.astype(BF16)
    lo = (r1 - mid.astype(F32)).astype(BF16)
    return hi, mid, lo


def _gla_prompt_kernel(x_ref, win_ref, wa1_ref, wa2_ref, ba_ref, ng_ref, wo_ref, g_ref, b_ref,
                       y_ref, so_ref, s_ref, proj_ref, u_ref, o_ref):
    ti = pl.program_id(1)
    ck = GLA_CHUNK

    @pl.when(ti == 0)
    def _():
        s_ref[...] = jnp.zeros_like(s_ref)

    x = x_ref[0]
    xb = x.astype(BF16)
    tm = x.shape[0]
    nck = tm // ck
    la = _gla_decay_logits(xb, wa1_ref, wa2_ref, ba_ref)
    c_v, c_g = 2 * GLA_DK, 2 * GLA_DK + GLA_DV
    proj_ref[:, :c_v] = _dot(xb, win_ref[:, :c_v])
    la_parts = _split3(la)

    r = lax.broadcasted_iota(jnp.int32, (ck, ck), 0)
    c = lax.broadcasted_iota(jnp.int32, (ck, ck), 1)
    causal = r >= c
    tril = causal.astype(BF16)

    tril3 = jnp.concatenate([tril] * 3 + [jnp.zeros_like(tril)], axis=1)
    zrows = jnp.zeros((ck, GLA_DK), BF16)
    bcs = [_dot(tril3, jnp.concatenate([part[ci * ck:(ci + 1) * ck] for part in la_parts] + [zrows],
                                       axis=0))
           for ci in range(nck)]

    proj_ref[:, c_v:c_g] = _dot(xb, win_ref[:, c_v:c_g])
    q_in, vb, att, dec = {}, {}, {}, {}
    for ci in range(nck):
        rows = slice(ci * ck, (ci + 1) * ck)
        bc = bcs[ci]
        for h in range(GLA_HEADS):
            bh = bc[:, h * GLA_DKH:(h + 1) * GLA_DKH]
            q = proj_ref[rows, h * GLA_DKH:(h + 1) * GLA_DKH] * (GLA_DKH ** -0.5)
            k = proj_ref[rows, GLA_DK + h * GLA_DKH:GLA_DK + (h + 1) * GLA_DKH]
            v = proj_ref[rows, 2 * GLA_DK + h * GLA_DVH:2 * GLA_DK + (h + 1) * GLA_DVH].astype(BF16)
            qi = (q * jnp.exp(bh)).astype(BF16)
            k_in = (k * jnp.exp(-bh)).astype(BF16)
            bl = bh[ck - 1:ck]
            kd = (k * jnp.exp(bl - bh)).astype(BF16)
            att[ci, h] = _dot_nt(qi, k_in)
            u_ref[ci, h] = _dot_tn(kd, v)
            q_in[ci, h], vb[ci, h], dec[ci, h] = qi, v, jnp.exp(bl)

    proj_ref[:, c_g:] = _dot(xb, win_ref[:, c_g:])
    state = [s_ref[h] for h in range(GLA_HEADS)]
    zpad_l = jnp.zeros((ck, ck), BF16)
    zpad_r = jnp.zeros((ck, GLA_DVH), BF16)
    for ci in range(nck):
        rows = slice(ci * ck, (ci + 1) * ck)
        for h in range(GLA_HEADS):
            am = jnp.where(causal, att[ci, h], 0.0).astype(BF16)
            lhs = jnp.concatenate([q_in[ci, h], am, zpad_l], axis=1)
            rhs = jnp.concatenate([state[h].astype(BF16), vb[ci, h], zpad_r], axis=0)
            o_ref[rows, h * GLA_DVH:(h + 1) * GLA_DVH] = _dot(lhs, rhs)
            dcol = jnp.transpose(jnp.broadcast_to(dec[ci, h], (GLA_DKH, GLA_DKH)))
            dfull = jnp.concatenate([dcol] * (GLA_DVH // GLA_DKH), axis=1)
            state[h] = dfull * state[h] + u_ref[ci, h]
    for h in range(GLA_HEADS):
        s_ref[h] = state[h]

    rt = tm // GLA_ROW_SPLITS
    for rs in range(GLA_ROW_SPLITS):
        rows = slice(rs * rt, (rs + 1) * rt)
        og = _gla_out_gate(o_ref[rows, :], proj_ref[rows, 2 * GLA_DK + GLA_DV:], ng_ref[...])
        z = ALPHA * x[rows] + _dot(og, wo_ref[...])
        y_ref[0, rows, :] = _layer_norm(z, g_ref[...], b_ref[...])

    @pl.when(ti == pl.num_programs(1) - 1)
    def _():
        so_ref[0] = s_ref[...]


def _gla_prompt(x, win, wa1, wa2, ba, ng, wo, g, b):
    bsz, seq, d = x.shape
    tm = min(GLA_TM, seq)
    nin = win.shape[1]
    st = (GLA_HEADS, GLA_DKH, GLA_DVH)
    return pl.pallas_call(
        _gla_prompt_kernel,
        grid=(bsz, seq // tm),
        in_specs=[
            pl.BlockSpec((1, tm, d), lambda bi, ti: (bi, ti, 0)),
            _resident(win.shape), _resident(wa1.shape), _resident(wa2.shape), _resident(ba.shape),
            _resident(ng.shape), _resident(wo.shape), _resident(g.shape), _resident(b.shape),
        ],
        out_specs=[pl.BlockSpec((1, tm, d), lambda bi, ti: (bi, ti, 0)),
                   pl.BlockSpec((1,) + st, lambda bi, ti: (bi, 0, 0, 0))],
        out_shape=[jax.ShapeDtypeStruct((bsz, seq, d), F32),
                   jax.ShapeDtypeStruct((bsz,) + st, F32)],
        scratch_shapes=[pltpu.VMEM(st, F32), pltpu.VMEM((tm, nin), F32),
                        pltpu.VMEM((tm // GLA_CHUNK,) + st, F32), pltpu.VMEM((tm, GLA_DV), F32)],
        compiler_params=_cparams(("arbitrary", "arbitrary")),
        name="gla_prompt",
    )(x, win, wa1, wa2, ba, ng, wo, g, b)


def _linear_kernel(x_ref, w_ref, b_ref, o_ref):
    o_ref[...] = _dot(x_ref[...].astype(BF16), w_ref[...]) + b_ref[...]


def _linear_cast_kernel(x_ref, w_ref, b_ref, o_ref, wb_ref):
    wb = w_ref[...].astype(BF16)
    wb_ref[...] = wb
    o_ref[...] = _dot(x_ref[...].astype(BF16), wb) + b_ref[...]


def _linear(x, w, bias):
    n, kdim = x.shape
    nout = w.shape[1]
    tn = next(t for t in (512, 256, 128) if nout % t == 0)
    emit = w.dtype != BF16
    wspec = pl.BlockSpec((kdim, tn), lambda j: (0, j))
    out_specs = [pl.BlockSpec((n, tn), lambda j: (0, j))]
    out_shape = [jax.ShapeDtypeStruct((n, nout), F32)]
    if emit:
        out_specs.append(wspec)
        out_shape.append(jax.ShapeDtypeStruct(w.shape, BF16))
    outs = pl.pallas_call(
        _linear_cast_kernel if emit else _linear_kernel,
        grid=(nout // tn,),
        in_specs=[pl.BlockSpec((n, kdim), lambda j: (0, 0)), wspec,
                  pl.BlockSpec((1, tn), lambda j: (0, j))],
        out_specs=out_specs,
        out_shape=out_shape,
        compiler_params=_cparams(("arbitrary",)),
        name="sample_linear",
    )(x, w, bias)
    return tuple(outs) if emit else outs[0]


def _linear_ln_kernel(h_ref, w_ref, r_ref, g_ref, b_ref, o_ref, wb_ref):
    wb = w_ref[...].astype(BF16)
    wb_ref[...] = wb
    z = ALPHA * r_ref[...] + _dot(h_ref[...].astype(BF16), wb)
    o_ref[...] = _layer_norm(z, g_ref[...], b_ref[...])


def _linear_ln(h, w, resid, g, b):
    n, d = resid.shape
    full = lambda a: pl.BlockSpec(a.shape, lambda i: (0,) * a.ndim)
    return pl.pallas_call(
        _linear_ln_kernel,
        grid=(1,),
        in_specs=[full(h), full(w), full(resid), full(g), full(b)],
        out_specs=[pl.BlockSpec((n, d), lambda i: (0, 0)), full(w)],
        out_shape=[jax.ShapeDtypeStruct((n, d), F32), jax.ShapeDtypeStruct(w.shape, BF16)],
        compiler_params=_cparams(("arbitrary",)),
        name="sample_linear_ln",
    )(h, w, resid, g, b)


def _attn_sample_kernel(q2_ref, kv_ref, ck_ref, cv_ref, cos_ref, sa_ref, sb_ref, sink_ref,
                        o2_ref, kn_ref, vn_ref):
    nseq, wbuf, _ = ck_ref.shape
    cos, sa, sb = cos_ref[...], sa_ref[...], sb_ref[...]
    q2 = (_rope_slab(q2_ref[...], cos, sa, sb) * ATTN_SCALE).astype(BF16)
    kv = kv_ref[...]
    k_all = _rope_slab(kv[:, :LANES], cos, sa, sb)
    v_all = kv[:, LANES:]
    last = lax.broadcasted_iota(jnp.int32, (wbuf, LANES), 0) == wbuf - 1
    sink = sink_ref[...]

    k_new, v_new = [], []
    for s in range(nseq):
        k_new.append(jnp.where(last, k_all[s:s + 1], pltpu.roll(ck_ref[s], wbuf - 1, 0)))
        v_new.append(jnp.where(last, v_all[s:s + 1], pltpu.roll(cv_ref[s], wbuf - 1, 0)))
        kn_ref[s] = k_new[s]
        vn_ref[s] = v_new[s]
    sc = [_dot_nt(q2[s * N_HEADS:(s + 1) * N_HEADS], k_new[s].astype(BF16)) for s in range(nseq)]
    ps = [_sink_softmax(sc[s], sink).astype(BF16) for s in range(nseq)]
    for s in range(nseq):
        o2_ref[s * N_HEADS:(s + 1) * N_HEADS, :] = _dot(ps[s], v_new[s].astype(BF16))


def _attn_sample(qkv, ck, cv, sinks):
    n = qkv.shape[0]
    wbuf = ck.shape[1]
    sb_ = SAMPLE_SEQ_BLOCK
    nq = N_HEADS * HEAD_DIM
    cos, sa, sb = _rope_tables(jnp.full((1,), PAST_LEN, dtype=F32))
    one = lambda a: pl.BlockSpec(a.shape, lambda i: (0,) * a.ndim)
    sink_col = sinks.reshape(N_HEADS, 1)
    q = qkv[:, :nq].reshape(n, N_HEADS, HEAD_DIM)
    zero = jnp.zeros_like(q)
    first_group = (jnp.arange(N_HEADS) < GROUP)[None, :, None]
    q2 = jnp.concatenate([jnp.where(first_group, q, zero), jnp.where(first_group, zero, q)], axis=-1)
    q2 = q2.reshape(n * N_HEADS, LANES)
    cache = pl.BlockSpec((sb_, wbuf, LANES), lambda i: (i, 0, 0))
    rows16 = pl.BlockSpec((sb_ * N_HEADS, LANES), lambda i: (i, 0))
    o2, k_s, v_s = pl.pallas_call(
        _attn_sample_kernel,
        grid=(n // sb_,),
        in_specs=[rows16, pl.BlockSpec((sb_, 2 * LANES), lambda i: (i, 0)), cache, cache,
                  one(cos), one(sa), one(sb), one(sink_col)],
        out_specs=[rows16, cache, cache],
        out_shape=[jax.ShapeDtypeStruct((n * N_HEADS, LANES), F32),
                   jax.ShapeDtypeStruct(ck.shape, F32), jax.ShapeDtypeStruct(cv.shape, F32)],
        compiler_params=_cparams(("arbitrary",)),
        name="attn_sample",
    )(q2, qkv[:, nq:], ck, cv, cos, sa, sb, sink_col)
    o2 = o2.reshape(n, N_HEADS, N_KV_HEADS, HEAD_DIM)
    o = jnp.where(first_group, o2[:, :, 0, :], o2[:, :, 1, :])
    return o.reshape(n, nq), k_s, v_s


def _gla_sample_kernel(proj_ref, z_ref, s0_ref, ng_ref, o_ref, sn_ref, oacc_ref):
    proj = proj_ref[...]
    la = jax.nn.log_sigmoid(z_ref[...]) * (1.0 / GLA_TAU)
    nseq = proj.shape[0]
    pad = jnp.zeros((LANES - nseq, GLA_DKH), F32)

    def cols(t):
        return jnp.transpose(jnp.concatenate([t, pad], axis=0))

    for h in range(GLA_HEADS):
        b = la[:, h * GLA_DKH:(h + 1) * GLA_DKH]
        q = proj[:, h * GLA_DKH:(h + 1) * GLA_DKH] * (GLA_DKH ** -0.5)
        k = proj[:, GLA_DK + h * GLA_DKH:GLA_DK + (h + 1) * GLA_DKH]
        v = proj[:, 2 * GLA_DK + h * GLA_DVH:2 * GLA_DK + (h + 1) * GLA_DVH]
        eb = jnp.exp(b)
        q_in = q * eb
        k_in = k * jnp.exp(-b)
        att = jnp.sum(q_in * k_in, axis=-1, keepdims=True)
        kd = k * jnp.exp(b - b)
        qc, kc, ec = cols(q_in), cols(kd), cols(eb)
        for s in range(nseq):
            s_old = s0_ref[s, h]
            vrow = v[s:s + 1]
            o_row = att[s:s + 1] * vrow + jnp.sum(qc[:, s:s + 1] * s_old, axis=0, keepdims=True)
            oacc_ref[s:s + 1, h * GLA_DVH:(h + 1) * GLA_DVH] = o_row
            sn_ref[s, h] = ec[:, s:s + 1] * s_old + kc[:, s:s + 1] * vrow
    og = _gla_out_gate(oacc_ref[...], proj[:, 2 * GLA_DK + GLA_DV:], ng_ref[...])
    o_ref[...] = og.astype(F32)


def _gla_sample(proj, z, s0, ng):
    n = proj.shape[0]
    sb_ = SAMPLE_SEQ_BLOCK
    st = pl.BlockSpec((sb_,) + s0.shape[1:], lambda i: (i, 0, 0, 0))
    return pl.pallas_call(
        _gla_sample_kernel,
        grid=(n // sb_,),
        in_specs=[pl.BlockSpec((sb_, proj.shape[1]), lambda i: (i, 0)),
                  pl.BlockSpec((sb_, z.shape[1]), lambda i: (i, 0)), st,
                  pl.BlockSpec(ng.shape, lambda i: (0, 0))],
        out_specs=[pl.BlockSpec((sb_, GLA_DV), lambda i: (i, 0)), st],
        out_shape=[jax.ShapeDtypeStruct((n, GLA_DV), F32), jax.ShapeDtypeStruct(s0.shape, F32)],
        scratch_shapes=[pltpu.VMEM((sb_, GLA_DV), F32)],
        compiler_params=_cparams(("arbitrary",)),
        name="gla_sample",
    )(proj, z, s0, ng)


def _row(v):
    return v.reshape(1, -1)


def kernel(x_prompt, x_sample, cache_k, cache_v, state_gla, state_conv, attn_w_qkv, attn_b_qkv, attn_sinks, attn_w_o, gla_w_in, gla_w_a1, gla_w_a2, gla_b_a, gla_norm_g, gla_w_o, ffn_w_up, ffn_conv_w, ffn_conv_b, ffn_w_down, ln_mix_g, ln_mix_b, ln_ffn_g, ln_ffn_b):
    bsz, seq, d = x_prompt.shape
    nsmp = x_sample.shape[0]
    xp = x_prompt
    xs = x_sample.reshape(nsmp, d)
    zeros = lambda n: jnp.zeros((1, n), F32)
    ffn_small = (ffn_conv_w, ffn_conv_b[:, None, :], ln_ffn_g[:, None, :], ln_ffn_b[:, None, :])

    kp_l, vp_l, ks_l, vs_l, sp_l, ss_l, cp_l, cs_l = [], [], [], [], [], [], [], []
    for i in range(DEPTH):
        j = i // 2
        g_mix, b_mix = _row(ln_mix_g[i]), _row(ln_mix_b[i])
        if i % 2 == 0:
            bqkv = _row(attn_b_qkv[j])
            wbuf = cache_k.shape[2]
            qkv_s, wqkv = _linear(xs, attn_w_qkv[j], bqkv)
            o_s, k_s, v_s = _attn_sample(qkv_s, cache_k[j].reshape(nsmp, wbuf, LANES),
                                         cache_v[j].reshape(nsmp, wbuf, LANES), attn_sinks[j])
            xs, wo = _linear_ln(o_s, attn_w_o[j], xs, g_mix, b_mix)
            ks_l.append(k_s.reshape(cache_k.shape[1:]))
            vs_l.append(v_s.reshape(cache_v.shape[1:]))

            xp, kp, vp = _attn_prompt(xp, wqkv, bqkv, attn_sinks[j], wo, g_mix, b_mix)
            kv_shape = (bsz, WINDOW, N_KV_HEADS, HEAD_DIM)
            kp_l.append(kp.reshape(kv_shape))
            vp_l.append(vp.reshape(kv_shape))
        else:
            wa1 = jnp.pad(gla_w_a1[j], ((0, 0), (0, LANES - GLA_RANK))).astype(BF16)
            wa2 = jnp.pad(gla_w_a2[j], ((0, LANES - GLA_RANK), (0, 0))).astype(BF16)
            ba = _row(gla_b_a[j])
            ng = _row(gla_norm_g[j])
            proj_s, win = _linear(xs, gla_w_in[j], zeros(gla_w_in.shape[2]))
            t1 = _linear(xs, wa1, zeros(LANES))
            z_s = _linear(t1, wa2, ba)
            o_s, s_s = _gla_sample(proj_s, z_s, state_gla[j], ng)
            xs, wo = _linear_ln(o_s, gla_w_o[j], xs, g_mix, b_mix)
            ss_l.append(s_s)

            xp, sp = _gla_prompt(xp, win, wa1, wa2, ba, ng, wo, g_mix, b_mix)
            sp_l.append(sp)

        xs, a_s, wa, wu, wd = _ffn_sample(xs, state_conv[i, :, 0], state_conv[i, :, 1], i,
                                          ffn_w_up, ffn_w_down, *ffn_small)
        cs_l.append(jnp.stack([state_conv[i, :, 1], a_s], axis=1))
        xp2, cp = _ffn_prompt(xp.reshape(bsz * seq, d), seq, i, wa, wu, wd, *ffn_small)
        xp = xp2.reshape(bsz, seq, d)
        cp_l.append(cp)

    return (xp, xs.reshape(x_sample.shape), jnp.stack(kp_l), jnp.stack(vp_l), jnp.stack(ks_l),
            jnp.stack(vs_l), jnp.stack(sp_l), jnp.stack(ss_l), jnp.stack(cp_l), jnp.stack(cs_l))
```

```python
import functools

import jax
import jax.numpy as jnp
from jax import lax
from jax.experimental import pallas as pl
from jax.experimental.pallas import tpu as pltpu

F32 = jnp.float32
BF16 = jnp.bfloat16

D_MODEL = 1024
DEPTH = 2
PAST_LEN = 16384
N_HEADS = 16
N_KV_HEADS = 2
HEAD_DIM = 64
GROUP = N_HEADS // N_KV_HEADS
WINDOW = 128
ROPE_THETA = 10000.0
ATTN_SCALE = HEAD_DIM ** -0.5
GLA_HEADS = 4
GLA_DK = D_MODEL // 2
GLA_DV = D_MODEL
GLA_DKH = GLA_DK // GLA_HEADS
GLA_DVH = GLA_DV // GLA_HEADS
GLA_RANK = 16
GLA_TAU = 16.0
GLA_CHUNK = 64
CONV_W = 3
ALPHA = (2 * DEPTH) ** 0.25
LN_EPS = 1e-5
NORM_EPS = 1e-6
NEG_INF = -1e30

LANES = 128
SUBLANES = 8
VMEM_LIMIT_BYTES = 56 * 1024 * 1024

ATTN_TQ = 2048
FFN_TM = 1024
FFN_TF = 256
FFN_ROW_SPLITS = 4
GLA_TM = 1024
GLA_ROW_SPLITS = 4
SAMPLE_SEQ_BLOCK = 8


def _cparams(sem):
    return pltpu.CompilerParams(dimension_semantics=sem, vmem_limit_bytes=VMEM_LIMIT_BYTES)


def _resident(shape):
    nd = len(shape)
    return pl.BlockSpec(shape, lambda *_: (0,) * nd, pipeline_mode=pl.Buffered(1))


def _resident_layer(stacked, layer):
    nd = stacked.ndim - 1
    return pl.BlockSpec((None,) + stacked.shape[1:], lambda *_: (layer,) + (0,) * nd,
                        pipeline_mode=pl.Buffered(1))


def _layer_norm(z, g, b):
    mu = jnp.mean(z, axis=-1, keepdims=True)
    zc = z - mu
    var = jnp.mean(zc * zc, axis=-1, keepdims=True)
    return zc * lax.rsqrt(var + LN_EPS) * g + b


def _gelu(c):
    return 0.5 * c * (1.0 + lax.erf(c * (2.0 ** -0.5)))


def _dot(a, b):
    return jnp.dot(a, b, preferred_element_type=F32)


def _dot_nt(a, b):
    return lax.dot_general(a, b, (((1,), (1,)), ((), ())), preferred_element_type=F32)


def _dot_tn(a, b):
    return lax.dot_general(a, b, (((0,), (0,)), ((), ())), preferred_element_type=F32)


def _rope_tables(pos):
    inv = 1.0 / (ROPE_THETA ** (jnp.arange(0, HEAD_DIM, 2, dtype=F32) / HEAD_DIM))
    ang = pos[:, None] * inv[None, :]
    cos, sin = jnp.cos(ang), jnp.sin(ang)
    zero = jnp.zeros_like(sin)
    cos_h = jnp.concatenate([cos, cos], -1)
    sa_h = jnp.concatenate([-sin, zero], -1)
    sb_h = jnp.concatenate([zero, sin], -1)
    two = lambda t: jnp.concatenate([t, t], -1)
    return two(cos_h), two(sa_h), two(sb_h)


def _rope_slab(s, cos, sa, sb):
    return (s * cos + pltpu.roll(s, LANES - HEAD_DIM // 2, 1) * sa
            + pltpu.roll(s, HEAD_DIM // 2, 1) * sb)


HALF = HEAD_DIM // 2


def _to_paired_columns(t, nslabs):
    parts = []
    for s in range(nslabs):
        c = s * LANES
        parts += [t[:, c:c + HALF], t[:, c + HEAD_DIM:c + HEAD_DIM + HALF],
                  t[:, c + HALF:c + HEAD_DIM], t[:, c + HEAD_DIM + HALF:c + LANES]]
    return jnp.concatenate(parts + [t[:, nslabs * LANES:]], axis=1)


def _rope_tables_paired(pos):
    inv = 1.0 / (ROPE_THETA ** (jnp.arange(0, HEAD_DIM, 2, dtype=F32) / HEAD_DIM))
    ang = pos[:, None] * inv[None, :]
    cos, sin = jnp.cos(ang), jnp.sin(ang)
    return (jnp.concatenate([cos] * 4, -1), jnp.concatenate([-sin, -sin, sin, sin], -1))


def _rope_paired(s, cos, sin_signed):
    return s * cos + pltpu.roll(s, HEAD_DIM, 1) * sin_signed


def _paired_split(t):
    lane = lax.broadcasted_iota(jnp.int32, t.shape, 1)
    on_a = (lane % HEAD_DIM) < HALF
    z = jnp.zeros_like(t)
    return (jnp.where(on_a, t, z), jnp.where(on_a, z, pltpu.roll(t, HALF, 1)),
            jnp.where(on_a, pltpu.roll(t, LANES - HALF, 1), z), jnp.where(on_a, z, t))


def _paired_to_standard(t):
    lane = lax.broadcasted_iota(jnp.int32, t.shape, 1)
    keep = (lane < HALF) | (lane >= LANES - HALF)
    return jnp.where(keep, t, jnp.where(lane < HEAD_DIM, pltpu.roll(t, LANES - HALF, 1),
                                        pltpu.roll(t, HALF, 1)))


def _sink_softmax(s, sink):
    m = jnp.maximum(jnp.max(s, axis=-1, keepdims=True), sink)
    p = jnp.exp(s - m)
    den = jnp.sum(p, axis=-1, keepdims=True) + jnp.exp(sink - m)
    return p * (1.0 / den)


def _lane_split_pair(t):
    lo = lax.broadcasted_iota(jnp.int32, t.shape, 1) < HEAD_DIM
    tr = pltpu.roll(t, HEAD_DIM, 1)
    z = jnp.zeros_like(t)
    return (jnp.where(lo, t, z), jnp.where(lo, z, tr), jnp.where(lo, tr, z), jnp.where(lo, z, t))


def _attn_prompt_kernel(sinks_ref, x_ref, cos_ref, sin_ref, wqkv_ref, bqkv_ref, wo_ref,
                        g_ref, b_ref, y_ref, kn_ref, vn_ref, kprev_ref, vprev_ref):
    ti = pl.program_id(1)
    nq = N_HEADS * HEAD_DIM
    w = WINDOW

    @pl.when(ti == 0)
    def _():
        kprev_ref[...] = jnp.zeros_like(kprev_ref)
        vprev_ref[...] = jnp.zeros_like(vprev_ref)

    nblk = x_ref.shape[1] // w
    nslab = nq // LANES
    spg = nslab // N_KV_HEADS

    t = lax.broadcasted_iota(jnp.int32, (w, w), 0)
    s = lax.broadcasted_iota(jnp.int32, (w, w), 1)
    upper = s > t
    has_prev = ti > 0

    def project(blk):
        rows = slice(blk * w, (blk + 1) * w)
        x = x_ref[0, rows, :]
        qkv = _dot(x.astype(BF16), wqkv_ref[...]) + bqkv_ref[...]
        cos, sin = cos_ref[rows, :], sin_ref[rows, :]
        k = _rope_paired(qkv[:, nq:nq + LANES], cos, sin)
        v = qkv[:, nq + LANES:nq + 2 * LANES]
        q = [(_rope_paired(qkv[:, j * LANES:(j + 1) * LANES], cos, sin) * ATTN_SCALE).astype(BF16)
             for j in range(nslab)]
        return dict(x=x, q=q, k=k, v=v, ks=_paired_split(k.astype(BF16)),
                    vs=_lane_split_pair(v.astype(BF16)))

    def finish(blk, st, pv):
        slabs = []
        for g in range(N_KV_HEADS):
            og = pv[g, 0] + pv[g, 1]
            slabs += [og[i * w:(i + 1) * w] for i in range(spg)]
        o = jnp.concatenate(slabs, axis=1).astype(BF16)
        z = ALPHA * st["x"] + _dot(o, wo_ref[...])
        y_ref[0, blk * w:(blk + 1) * w, :] = _layer_norm(z, g_ref[...], b_ref[...])

    units = [(g, par) for g in range(N_KV_HEADS) for par in range(2)]
    prev = dict(ks=_paired_split(kprev_ref[...].astype(BF16)),
                vs=_lane_split_pair(vprev_ref[...].astype(BF16)))
    stages = {-1: prev, 0: project(0)}
    pvs = {}
    for blk in range(nblk):
        if blk + 1 < nblk:
            stages[blk + 1] = project(blk + 1)
        st, pst = stages[blk], stages[blk - 1]

        def scores(u):
            g, par = u
            qg = jnp.concatenate([st["q"][g * spg + i] for i in range(spg)], axis=0)
            kmat = jnp.concatenate([pst["ks"][2 * g + par], st["ks"][2 * g + par]], axis=0)
            return _dot_nt(qg, kmat)

        pv = {}
        sc_nxt = scores(units[0])
        for n, (g, par) in enumerate(units):
            sc = sc_nxt
            if n + 1 < len(units):
                sc_nxt = scores(units[n + 1])
            ps = []
            for i in range(spg):
                head = 2 * (g * spg + i) + par
                s_prev = sc[i * w:(i + 1) * w, :w]
                if blk == 0:
                    s_prev = jnp.where(has_prev, s_prev, NEG_INF)
                p = _sink_softmax(jnp.where(upper, s_prev, sc[i * w:(i + 1) * w, w:]), sinks_ref[head])
                zero = jnp.zeros_like(p)
                ps.append(jnp.concatenate([jnp.where(upper, p, zero), jnp.where(upper, zero, p)],
                                          axis=1).astype(BF16))
            vmat = jnp.concatenate([pst["vs"][2 * g + par], st["vs"][2 * g + par]], axis=0)
            pv[g, par] = _dot(jnp.concatenate(ps, axis=0), vmat)
        pvs[blk] = pv
        if blk >= 1:
            finish(blk - 1, stages[blk - 1], pvs[blk - 1])
    finish(nblk - 1, stages[nblk - 1], pvs[nblk - 1])

    k_last, v_last = stages[nblk - 1]["k"], stages[nblk - 1]["v"]
    kprev_ref[...] = k_last
    vprev_ref[...] = v_last

    @pl.when(ti == pl.num_programs(1) - 1)
    def _():
        kn_ref[0] = _paired_to_standard(k_last)
        vn_ref[0] = v_last


def _attn_prompt(x, wqkv, bqkv, sinks, wo, g, b):
    bsz, seq, d = x.shape
    tq = min(ATTN_TQ, seq)
    nqkv = wqkv.shape[1]
    nq = N_HEADS * HEAD_DIM
    nslabs = (nq + LANES) // LANES
    wqkv, bqkv = _to_paired_columns(wqkv, nslabs), _to_paired_columns(bqkv, nslabs)
    cos, sin = _rope_tables_paired(jnp.arange(seq, dtype=F32))
    tab = pl.BlockSpec((tq, LANES), lambda bi, ti: (ti, 0))
    kv_out = pl.BlockSpec((1, WINDOW, LANES), lambda bi, ti: (bi, 0, 0))
    return pl.pallas_call(
        _attn_prompt_kernel,
        grid=(bsz, seq // tq),
        in_specs=[
            pl.BlockSpec(memory_space=pltpu.SMEM),
            pl.BlockSpec((1, tq, d), lambda bi, ti: (bi, ti, 0)),
            tab, tab,
            _resident((d, nqkv)), _resident((1, nqkv)), _resident((d, d)),
            _resident((1, d)), _resident((1, d)),
        ],
        out_specs=[pl.BlockSpec((1, tq, d), lambda bi, ti: (bi, ti, 0)), kv_out, kv_out],
        out_shape=[jax.ShapeDtypeStruct((bsz, seq, d), F32),
                   jax.ShapeDtypeStruct((bsz, WINDOW, LANES), F32),
                   jax.ShapeDtypeStruct((bsz, WINDOW, LANES), F32)],
        scratch_shapes=[pltpu.VMEM((WINDOW, LANES), F32), pltpu.VMEM((WINDOW, LANES), F32)],
        compiler_params=_cparams(("arbitrary", "arbitrary")),
        name="attn_prompt",
    )(sinks, x, cos, sin, wqkv, bqkv, wo, g, b)


def _conv_gate(a, a1, a2, u, cw, cb):
    c = cb + cw[0:1] * a2 + cw[1:2] * a1 + cw[2:3] * a
    return (_gelu(c) * u).astype(BF16)


def _ffn_prompt_kernel(x_ref, wa_ref, wu_ref, wd_ref, cw_ref, cb_ref, g_ref, b_ref,
                       y_ref, cs_ref, carry_ref, h_ref, *, tiles_per_seq):
    i = pl.program_id(0)
    pos = i % tiles_per_seq

    @pl.when(pos == 0)
    def _():
        carry_ref[...] = jnp.zeros_like(carry_ref)

    x = x_ref[...]
    xb = x.astype(BF16)
    tm = x.shape[0]
    dff = wd_ref.shape[0]
    tf = FFN_TF
    row = lax.broadcasted_iota(jnp.int32, (SUBLANES, tf), 0)
    for j in range(dff // tf):
        sl = slice(j * tf, (j + 1) * tf)
        a = _dot(xb, wa_ref[:, sl])
        u = _dot(xb, wu_ref[:, sl])
        prev = carry_ref[:, sl]
        p1 = prev[SUBLANES - 1:SUBLANES]
        p2 = prev[SUBLANES - 2:SUBLANES - 1]
        a1 = pltpu.roll(a, 1, 0)
        a2 = pltpu.roll(a, 2, 0)
        a1_top = jnp.where(row == 0, p1, a1[:SUBLANES])
        a2_top = jnp.where(row == 0, p2, jnp.where(row == 1, p1, a2[:SUBLANES]))
        a1 = jnp.concatenate([a1_top, a1[SUBLANES:]], axis=0)
        a2 = jnp.concatenate([a2_top, a2[SUBLANES:]], axis=0)
        carry_ref[:, sl] = a[tm - SUBLANES:tm]
        h_ref[:, sl] = _conv_gate(a, a1, a2, u, cw_ref[:, sl], cb_ref[:, sl])

    rt = tm // FFN_ROW_SPLITS
    for r in range(FFN_ROW_SPLITS):
        rows = slice(r * rt, (r + 1) * rt)
        z = ALPHA * x[rows] + _dot(h_ref[rows, :], wd_ref[...])
        y_ref[rows, :] = _layer_norm(z, g_ref[...], b_ref[...])

    @pl.when(pos == tiles_per_seq - 1)
    def _():
        cs_ref[0] = carry_ref[SUBLANES - (CONV_W - 1):SUBLANES, :]


def _ffn_prompt(x, seq, layer, wa, wu, wd, cw, cb, g, b):
    n, d = x.shape
    tm = min(FFN_TM, seq)
    tiles_per_seq = seq // tm
    dff = wd.shape[0]
    return pl.pallas_call(
        functools.partial(_ffn_prompt_kernel, tiles_per_seq=tiles_per_seq),
        grid=(n // tm,),
        in_specs=[pl.BlockSpec((tm, d), lambda i: (i, 0))]
        + [_resident(a.shape) for a in (wa, wu, wd)]
        + [_resident_layer(a, layer) for a in (cw, cb, g, b)],
        out_specs=[pl.BlockSpec((tm, d), lambda i: (i, 0)),
                   pl.BlockSpec((1, CONV_W - 1, dff), lambda i: (i // tiles_per_seq, 0, 0))],
        out_shape=[jax.ShapeDtypeStruct((n, d), F32),
                   jax.ShapeDtypeStruct((n // seq, CONV_W - 1, dff), F32)],
        scratch_shapes=[pltpu.VMEM((SUBLANES, dff), F32), pltpu.VMEM((tm, dff), BF16)],
        compiler_params=_cparams(("arbitrary",)),
        name="ffn_prompt",
    )(x, wa, wu, wd, cw, cb, g, b)


def _ffn_sample_kernel(x_ref, s0_ref, s1_ref, wa_ref, wu_ref, wd_ref, cw_ref, cb_ref, g_ref, b_ref,
                       y_ref, a_ref, wab_ref, wub_ref, wdb_ref, acc_ref):
    j = pl.program_id(0)
    x = x_ref[...]
    xb = x.astype(BF16)
    wa, wu, wd = wa_ref[...].astype(BF16), wu_ref[...].astype(BF16), wd_ref[...].astype(BF16)
    wab_ref[...] = wa
    wub_ref[...] = wu
    wdb_ref[...] = wd
    a = _dot(xb, wa)
    u = _dot(xb, wu)
    a_ref[...] = a
    h = _conv_gate(a, s1_ref[...], s0_ref[...], u, cw_ref[...], cb_ref[...])
    dn = _dot(h, wd)

    @pl.when(j == 0)
    def _():
        acc_ref[...] = dn

    @pl.when(j > 0)
    def _():
        acc_ref[...] += dn

    @pl.when(j == pl.num_programs(0) - 1)
    def _():
        y_ref[...] = _layer_norm(ALPHA * x + acc_ref[...], g_ref[...], b_ref[...])


def _ffn_sample(x, s0, s1, layer, wup, wd, cw, cb, g, b):
    n, d = x.shape
    dff = wd.shape[1]
    tf = FFN_TF
    nchunk = dff // tf
    col = pl.BlockSpec((n, tf), lambda j: (0, j))
    return pl.pallas_call(
        _ffn_sample_kernel,
        grid=(nchunk,),
        in_specs=[
            pl.BlockSpec((n, d), lambda j: (0, 0)), col, col,
            pl.BlockSpec((None, d, tf), lambda j: (layer, 0, j)),
            pl.BlockSpec((None, d, tf), lambda j: (layer, 0, j + nchunk)),
            pl.BlockSpec((None, tf, d), lambda j: (layer, j, 0)),
            pl.BlockSpec((None, CONV_W, tf), lambda j: (layer, 0, j)),
            pl.BlockSpec((None, 1, tf), lambda j: (layer, 0, j)),
            pl.BlockSpec((None, 1, d), lambda j: (layer, 0, 0)),
            pl.BlockSpec((None, 1, d), lambda j: (layer, 0, 0)),
        ],
        out_specs=[pl.BlockSpec((n, d), lambda j: (0, 0)), col,
                   pl.BlockSpec((d, tf), lambda j: (0, j)), pl.BlockSpec((d, tf), lambda j: (0, j)),
                   pl.BlockSpec((tf, d), lambda j: (j, 0))],
        out_shape=[jax.ShapeDtypeStruct((n, d), F32), jax.ShapeDtypeStruct((n, dff), F32),
                   jax.ShapeDtypeStruct((d, dff), BF16), jax.ShapeDtypeStruct((d, dff), BF16),
                   jax.ShapeDtypeStruct((dff, d), BF16)],
        scratch_shapes=[pltpu.VMEM((n, d), F32)],
        compiler_params=_cparams(("arbitrary",)),
        name="ffn_sample",
    )(x, s0, s1, wup, wup, wd, cw, cb, g, b)


def _gla_decay_logits(xb, wa1_ref, wa2_ref, ba_ref):
    t1 = _dot(xb, wa1_ref[...])
    z = _dot(t1.astype(BF16), wa2_ref[...]) + ba_ref[...]
    return jax.nn.log_sigmoid(z) * (1.0 / GLA_TAU)


def _gla_out_gate(o, gate, ng):
    outs = []
    for h in range(GLA_HEADS):
        oh = o[:, h * GLA_DVH:(h + 1) * GLA_DVH]
        ms = jnp.mean(oh * oh, axis=-1, keepdims=True)
        outs.append(oh * lax.rsqrt(ms + NORM_EPS) * ng)
    on = jnp.concatenate(outs, axis=1)
    return (on * jax.nn.silu(gate)).astype(BF16)


def _split3(t):
    hi = t.astype(BF16)
    r1 = t - hi.astype(F32)
    mid = r1.astype(BF16)
    lo = (r1 - mid.astype(F32)).astype(BF16)
    return hi, mid, lo


def _gla_prompt_kernel(x_ref, win_ref, wa1_ref, wa2_ref, ba_ref, ng_ref, wo_ref, g_ref, b_ref,
                       y_ref, so_ref, s_ref, proj_ref, u_ref, o_ref):
    ti = pl.program_id(1)
    ck = GLA_CHUNK

    @pl.when(ti == 0)
    def _():
        s_ref[...] = jnp.zeros_like(s_ref)

    x = x_ref[0]
    xb = x.astype(BF16)
    tm = x.shape[0]
    nck = tm // ck
    la = _gla_decay_logits(xb, wa1_ref, wa2_ref, ba_ref)
    c_v, c_g = 2 * GLA_DK, 2 * GLA_DK + GLA_DV
    proj_ref[:, :c_v] = _dot(xb, win_ref[:, :c_v])
    la_parts = _split3(la)

    r = lax.broadcasted_iota(jnp.int32, (ck, ck), 0)
    c = lax.broadcasted_iota(jnp.int32, (ck, ck), 1)
    causal = r >= c
    tril = causal.astype(BF16)

    tril3 = jnp.concatenate([tril] * 3 + [jnp.zeros_like(tril)], axis=1)
    zrows = jnp.zeros((ck, GLA_DK), BF16)
    bcs = [_dot(tril3, jnp.concatenate([part[ci * ck:(ci + 1) * ck] for part in la_parts] + [zrows],
                                       axis=0))
           for ci in range(nck)]

    proj_ref[:, c_v:c_g] = _dot(xb, win_ref[:, c_v:c_g])
    q_in, vb, att, dec = {}, {}, {}, {}
    for ci in range(nck):
        rows = slice(ci * ck, (ci + 1) * ck)
        bc = bcs[ci]
        for h in range(GLA_HEADS):
            bh = bc[:, h * GLA_DKH:(h + 1) * GLA_DKH]
            q = proj_ref[rows, h * GLA_DKH:(h + 1) * GLA_DKH] * (GLA_DKH ** -0.5)
            k = proj_ref[rows, GLA_DK + h * GLA_DKH:GLA_DK + (h + 1) * GLA_DKH]
            v = proj_ref[rows, 2 * GLA_DK + h * GLA_DVH:2 * GLA_DK + (h + 1) * GLA_DVH].astype(BF16)
            qi = (q * jnp.exp(bh)).astype(BF16)
            k_in = (k * jnp.exp(-bh)).astype(BF16)
            bl = bh[ck - 1:ck]
            kd = (k * jnp.exp(bl - bh)).astype(BF16)
            att[ci, h] = _dot_nt(qi, k_in)
            u_ref[ci, h] = _dot_tn(kd, v)
            q_in[ci, h], vb[ci, h], dec[ci, h] = qi, v, jnp.exp(bl)

    proj_ref[:, c_g:] = _dot(xb, win_ref[:, c_g:])
    state = [s_ref[h] for h in range(GLA_HEADS)]
    zpad_l = jnp.zeros((ck, ck), BF16)
    zpad_r = jnp.zeros((ck, GLA_DVH), BF16)
    for ci in range(nck):
        rows = slice(ci * ck, (ci + 1) * ck)
        for h in range(GLA_HEADS):
            am = jnp.where(causal, att[ci, h], 0.0).astype(BF16)
            lhs = jnp.concatenate([q_in[ci, h], am, zpad_l], axis=1)
            rhs = jnp.concatenate([state[h].astype(BF16), vb[ci, h], zpad_r], axis=0)
            o_ref[rows, h * GLA_DVH:(h + 1) * GLA_DVH] = _dot(lhs, rhs)
            dcol = jnp.transpose(jnp.broadcast_to(dec[ci, h], (GLA_DKH, GLA_DKH)))
            dfull = jnp.concatenate([dcol] * (GLA_DVH // GLA_DKH), axis=1)
            state[h] = dfull * state[h] + u_ref[ci, h]
    for h in range(GLA_HEADS):
        s_ref[h] = state[h]

    rt = tm // GLA_ROW_SPLITS
    for rs in range(GLA_ROW_SPLITS):
        rows = slice(rs * rt, (rs + 1) * rt)
        og = _gla_out_gate(o_ref[rows, :], proj_ref[rows, 2 * GLA_DK + GLA_DV:], ng_ref[...])
        z = ALPHA * x[rows] + _dot(og, wo_ref[...])
        y_ref[0, rows, :] = _layer_norm(z, g_ref[...], b_ref[...])

    @pl.when(ti == pl.num_programs(1) - 1)
    def _():
        so_ref[0] = s_ref[...]


def _gla_prompt(x, win, wa1, wa2, ba, ng, wo, g, b):
    bsz, seq, d = x.shape
    tm = min(GLA_TM, seq)
    nin = win.shape[1]
    st = (GLA_HEADS, GLA_DKH, GLA_DVH)
    return pl.pallas_call(
        _gla_prompt_kernel,
        grid=(bsz, seq // tm),
        in_specs=[
            pl.BlockSpec((1, tm, d), lambda bi, ti: (bi, ti, 0)),
            _resident(win.shape), _resident(wa1.shape), _resident(wa2.shape), _resident(ba.shape),
            _resident(ng.shape), _resident(wo.shape), _resident(g.shape), _resident(b.shape),
        ],
        out_specs=[pl.BlockSpec((1, tm, d), lambda bi, ti: (bi, ti, 0)),
                   pl.BlockSpec((1,) + st, lambda bi, ti: (bi, 0, 0, 0))],
        out_shape=[jax.ShapeDtypeStruct((bsz, seq, d), F32),
                   jax.ShapeDtypeStruct((bsz,) + st, F32)],
        scratch_shapes=[pltpu.VMEM(st, F32), pltpu.VMEM((tm, nin), F32),
                        pltpu.VMEM((tm // GLA_CHUNK,) + st, F32), pltpu.VMEM((tm, GLA_DV), F32)],
        compiler_params=_cparams(("arbitrary", "arbitrary")),
        name="gla_prompt",
    )(x, win, wa1, wa2, ba, ng, wo, g, b)


def _linear_kernel(x_ref, w_ref, b_ref, o_ref):
    o_ref[...] = _dot(x_ref[...].astype(BF16), w_ref[...]) + b_ref[...]


def _linear_cast_kernel(x_ref, w_ref, b_ref, o_ref, wb_ref):
    wb = w_ref[...].astype(BF16)
    wb_ref[...] = wb
    o_ref[...] = _dot(x_ref[...].astype(BF16), wb) + b_ref[...]


def _linear(x, w, bias):
    n, kdim = x.shape
    nout = w.shape[1]
    tn = next(t for t in (512, 256, 128) if nout % t == 0)
    emit = w.dtype != BF16
    wspec = pl.BlockSpec((kdim, tn), lambda j: (0, j))
    out_specs = [pl.BlockSpec((n, tn), lambda j: (0, j))]
    out_shape = [jax.ShapeDtypeStruct((n, nout), F32)]
    if emit:
        out_specs.append(wspec)
        out_shape.append(jax.ShapeDtypeStruct(w.shape, BF16))
    outs = pl.pallas_call(
        _linear_cast_kernel if emit else _linear_kernel,
        grid=(nout // tn,),
        in_specs=[pl.BlockSpec((n, kdim), lambda j: (0, 0)), wspec,
                  pl.BlockSpec((1, tn), lambda j: (0, j))],
        out_specs=out_specs,
        out_shape=out_shape,
        compiler_params=_cparams(("arbitrary",)),
        name="sample_linear",
    )(x, w, bias)
    return tuple(outs) if emit else outs[0]


def _linear_ln_kernel(h_ref, w_ref, r_ref, g_ref, b_ref, o_ref, wb_ref):
    wb = w_ref[...].astype(BF16)
    wb_ref[...] = wb
    z = ALPHA * r_ref[...] + _dot(h_ref[...].astype(BF16), wb)
    o_ref[...] = _layer_norm(z, g_ref[...], b_ref[...])


def _linear_ln(h, w, resid, g, b):
    n, d = resid.shape
    full = lambda a: pl.BlockSpec(a.shape, lambda i: (0,) * a.ndim)
    return pl.pallas_call(
        _linear_ln_kernel,
        grid=(1,),
        in_specs=[full(h), full(w), full(resid), full(g), full(b)],
        out_specs=[pl.BlockSpec((n, d), lambda i: (0, 0)), full(w)],
        out_shape=[jax.ShapeDtypeStruct((n, d), F32), jax.ShapeDtypeStruct(w.shape, BF16)],
        compiler_params=_cparams(("arbitrary",)),
        name="sample_linear_ln",
    )(h, w, resid, g, b)


def _attn_sample_kernel(q2_ref, kv_ref, ck_ref, cv_ref, cos_ref, sa_ref, sb_ref, sink_ref,
                        o2_ref, kn_ref, vn_ref):
    nseq, wbuf, _ = ck_ref.shape
    cos, sa, sb = cos_ref[...], sa_ref[...], sb_ref[...]
    q2 = (_rope_slab(q2_ref[...], cos, sa, sb) * ATTN_SCALE).astype(BF16)
    kv = kv_ref[...]
    k_all = _rope_slab(kv[:, :LANES], cos, sa, sb)
    v_all = kv[:, LANES:]
    last = lax.broadcasted_iota(jnp.int32, (wbuf, LANES), 0) == wbuf - 1
    sink = sink_ref[...]

    k_new, v_new = [], []
    for s in range(nseq):
        k_new.append(jnp.where(last, k_all[s:s + 1], pltpu.roll(ck_ref[s], wbuf - 1, 0)))
        v_new.append(jnp.where(last, v_all[s:s + 1], pltpu.roll(cv_ref[s], wbuf - 1, 0)))
        kn_ref[s] = k_new[s]
        vn_ref[s] = v_new[s]
    sc = [_dot_nt(q2[s * N_HEADS:(s + 1) * N_HEADS], k_new[s].astype(BF16)) for s in range(nseq)]
    ps = [_sink_softmax(sc[s], sink).astype(BF16) for s in range(nseq)]
    for s in range(nseq):
        o2_ref[s * N_HEADS:(s + 1) * N_HEADS, :] = _dot(ps[s], v_new[s].astype(BF16))


def _attn_sample(qkv, ck, cv, sinks):
    n = qkv.shape[0]
    wbuf = ck.shape[1]
    sb_ = SAMPLE_SEQ_BLOCK
    nq = N_HEADS * HEAD_DIM
    cos, sa, sb = _rope_tables(jnp.full((1,), PAST_LEN, dtype=F32))
    one = lambda a: pl.BlockSpec(a.shape, lambda i: (0,) * a.ndim)
    sink_col = sinks.reshape(N_HEADS, 1)
    q = qkv[:, :nq].reshape(n, N_HEADS, HEAD_DIM)
    zero = jnp.zeros_like(q)
    first_group = (jnp.arange(N_HEADS) < GROUP)[None, :, None]
    q2 = jnp.concatenate([jnp.where(first_group, q, zero), jnp.where(first_group, zero, q)], axis=-1)
    q2 = q2.reshape(n * N_HEADS, LANES)
    cache = pl.BlockSpec((sb_, wbuf, LANES), lambda i: (i, 0, 0))
    rows16 = pl.BlockSpec((sb_ * N_HEADS, LANES), lambda i: (i, 0))
    o2, k_s, v_s = pl.pallas_call(
        _attn_sample_kernel,
        grid=(n // sb_,),
        in_specs=[rows16, pl.BlockSpec((sb_, 2 * LANES), lambda i: (i, 0)), cache, cache,
                  one(cos), one(sa), one(sb), one(sink_col)],
        out_specs=[rows16, cache, cache],
        out_shape=[jax.ShapeDtypeStruct((n * N_HEADS, LANES), F32),
                   jax.ShapeDtypeStruct(ck.shape, F32), jax.ShapeDtypeStruct(cv.shape, F32)],
        compiler_params=_cparams(("arbitrary",)),
        name="attn_sample",
    )(q2, qkv[:, nq:], ck, cv, cos, sa, sb, sink_col)
    o2 = o2.reshape(n, N_HEADS, N_KV_HEADS, HEAD_DIM)
    o = jnp.where(first_group, o2[:, :, 0, :], o2[:, :, 1, :])
    return o.reshape(n, nq), k_s, v_s


def _gla_sample_kernel(proj_ref, z_ref, s0_ref, ng_ref, o_ref, sn_ref, oacc_ref):
    proj = proj_ref[...]
    la = jax.nn.log_sigmoid(z_ref[...]) * (1.0 / GLA_TAU)
    nseq = proj.shape[0]
    pad = jnp.zeros((LANES - nseq, GLA_DKH), F32)

    def cols(t):
        return jnp.transpose(jnp.concatenate([t, pad], axis=0))

    for h in range(GLA_HEADS):
        b = la[:, h * GLA_DKH:(h + 1) * GLA_DKH]
        q = proj[:, h * GLA_DKH:(h + 1) * GLA_DKH] * (GLA_DKH ** -0.5)
        k = proj[:, GLA_DK + h * GLA_DKH:GLA_DK + (h + 1) * GLA_DKH]
        v = proj[:, 2 * GLA_DK + h * GLA_DVH:2 * GLA_DK + (h + 1) * GLA_DVH]
        eb = jnp.exp(b)
        q_in = q * eb
        k_in = k * jnp.exp(-b)
        att = jnp.sum(q_in * k_in, axis=-1, keepdims=True)
        kd = k * jnp.exp(b - b)
        qc, kc, ec = cols(q_in), cols(kd), cols(eb)
        for s in range(nseq):
            s_old = s0_ref[s, h]
            vrow = v[s:s + 1]
            o_row = att[s:s + 1] * vrow + jnp.sum(qc[:, s:s + 1] * s_old, axis=0, keepdims=True)
            oacc_ref[s:s + 1, h * GLA_DVH:(h + 1) * GLA_DVH] = o_row
            sn_ref[s, h] = ec[:, s:s + 1] * s_old + kc[:, s:s + 1] * vrow
    og = _gla_out_gate(oacc_ref[...], proj[:, 2 * GLA_DK + GLA_DV:], ng_ref[...])
    o_ref[...] = og.astype(F32)


def _gla_sample(proj, z, s0, ng):
    n = proj.shape[0]
    sb_ = SAMPLE_SEQ_BLOCK
    st = pl.BlockSpec((sb_,) + s0.shape[1:], lambda i: (i, 0, 0, 0))
    return pl.pallas_call(
        _gla_sample_kernel,
        grid=(n // sb_,),
        in_specs=[pl.BlockSpec((sb_, proj.shape[1]), lambda i: (i, 0)),
                  pl.BlockSpec((sb_, z.shape[1]), lambda i: (i, 0)), st,
                  pl.BlockSpec(ng.shape, lambda i: (0, 0))],
        out_specs=[pl.BlockSpec((sb_, GLA_DV), lambda i: (i, 0)), st],
        out_shape=[jax.ShapeDtypeStruct((n, GLA_DV), F32), jax.ShapeDtypeStruct(s0.shape, F32)],
        scratch_shapes=[pltpu.VMEM((sb_, GLA_DV), F32)],
        compiler_params=_cparams(("arbitrary",)),
        name="gla_sample",
    )(proj, z, s0, ng)


def _row(v):
    return v.reshape(1, -1)


def kernel(x_prompt, x_sample, cache_k, cache_v, state_gla, state_conv, attn_w_qkv, attn_b_qkv, attn_sinks, attn_w_o, gla_w_in, gla_w_a1, gla_w_a2, gla_b_a, gla_norm_g, gla_w_o, ffn_w_up, ffn_conv_w, ffn_conv_b, ffn_w_down, ln_mix_g, ln_mix_b, ln_ffn_g, ln_ffn_b):
    bsz, seq, d = x_prompt.shape
    nsmp = x_sample.shape[0]
    xp = x_prompt
    xs = x_sample.reshape(nsmp, d)
    zeros = lambda n: jnp.zeros((1, n), F32)
    ffn_small = (ffn_conv_w, ffn_conv_b[:, None, :], ln_ffn_g[:, None, :], ln_ffn_b[:, None, :])

    kp_l, vp_l, ks_l, vs_l, sp_l, ss_l, cp_l, cs_l = [], [], [], [], [], [], [], []
    for i in range(DEPTH):
        j = i // 2
        g_mix, b_mix = _row(ln_mix_g[i]), _row(ln_mix_b[i])
        if i % 2 == 0:
            bqkv = _row(attn_b_qkv[j])
            wbuf = cache_k.shape[2]
            qkv_s, wqkv = _linear(xs, attn_w_qkv[j], bqkv)
            o_s, k_s, v_s = _attn_sample(qkv_s, cache_k[j].reshape(nsmp, wbuf, LANES),
                                         cache_v[j].reshape(nsmp, wbuf, LANES), attn_sinks[j])
            xs, wo = _linear_ln(o_s, attn_w_o[j], xs, g_mix, b_mix)
            ks_l.append(k_s.reshape(cache_k.shape[1:]))
            vs_l.append(v_s.reshape(cache_v.shape[1:]))

            xp, kp, vp = _attn_prompt(xp, wqkv, bqkv, attn_sinks[j], wo, g_mix, b_mix)
            kv_shape = (bsz, WINDOW, N_KV_HEADS, HEAD_DIM)
            kp_l.append(kp.reshape(kv_shape))
            vp_l.append(vp.reshape(kv_shape))
        else:
            wa1 = jnp.pad(gla_w_a1[j], ((0, 0), (0, LANES - GLA_RANK))).astype(BF16)
            wa2 = jnp.pad(gla_w_a2[j], ((0, LANES - GLA_RANK), (0, 0))).astype(BF16)
            ba = _row(gla_b_a[j])
            ng = _row(gla_norm_g[j])
            proj_s, win = _linear(xs, gla_w_in[j], zeros(gla_w_in.shape[2]))
            t1 = _linear(xs, wa1, zeros(LANES))
            z_s = _linear(t1, wa2, ba)
            o_s, s_s = _gla_sample(proj_s, z_s, state_gla[j], ng)
            xs, wo = _linear_ln(o_s, gla_w_o[j], xs, g_mix, b_mix)
            ss_l.append(s_s)

            xp, sp = _gla_prompt(xp, win, wa1, wa2, ba, ng, wo, g_mix, b_mix)
            sp_l.append(sp)

        xs, a_s, wa, wu, wd = _ffn_sample(xs, state_conv[i, :, 0], state_conv[i, :, 1], i,
                                          ffn_w_up, ffn_w_down, *ffn_small)
        cs_l.append(jnp.stack([state_conv[i, :, 1], a_s], axis=1))
        xp2, cp = _ffn_prompt(xp.reshape(bsz * seq, d), seq, i, wa, wu, wd, *ffn_small)
        xp = xp2.reshape(bsz, seq, d)
        cp_l.append(cp)

    return (xp, xs.reshape(x_sample.shape), jnp.stack(kp_l), jnp.stack(vp_l), jnp.stack(ks_l),
            jnp.stack(vs_l), jnp.stack(sp_l), jnp.stack(ss_l), jnp.stack(cp_l), jnp.stack(cs_l))
```

```python
import functools

import jax
import jax.numpy as jnp
from jax import lax
from jax.experimental import pallas as pl
from jax.experimental.pallas import tpu as pltpu

F32 = jnp.float32
BF16 = jnp.bfloat16

D_MODEL = 1024
DEPTH = 2
PAST_LEN = 16384
N_HEADS = 16
N_KV_HEADS = 2
HEAD_DIM = 64
GROUP = N_HEADS // N_KV_HEADS
WINDOW = 128
ROPE_THETA = 10000.0
ATTN_SCALE = HEAD_DIM ** -0.5
GLA_HEADS = 4
GLA_DK = D_MODEL // 2
GLA_DV = D_MODEL
GLA_DKH = GLA_DK // GLA_HEADS
GLA_DVH = GLA_DV // GLA_HEADS
GLA_RANK = 16
GLA_TAU = 16.0
GLA_CHUNK = 64
D_FF = 2816
CONV_W = 3
ALPHA = (2 * DEPTH) ** 0.25
LN_EPS = 1e-5
NORM_EPS = 1e-6
NEG_INF = -1e30

LANES = 128
SUBLANES = 8
VMEM_LIMIT_BYTES = 56 * 1024 * 1024

ATTN_TQ = 2048
FFN_TM = 1024
FFN_TF = 256
FFN_ROW_SPLITS = 4
GLA_TM = 1024
GLA_ROW_SPLITS = 4
SAMPLE_SEQ_BLOCK = 16


def _cparams(sem):
    return pltpu.CompilerParams(dimension_semantics=sem, vmem_limit_bytes=VMEM_LIMIT_BYTES)


def _resident(shape):
    nd = len(shape)
    return pl.BlockSpec(shape, lambda *_: (0,) * nd, pipeline_mode=pl.Buffered(1))


def _resident_layer(stacked, layer):
    nd = stacked.ndim - 1
    return pl.BlockSpec((None,) + stacked.shape[1:], lambda *_: (layer,) + (0,) * nd,
                        pipeline_mode=pl.Buffered(1))


def _layer_norm(z, g, b):
    mu = jnp.mean(z, axis=-1, keepdims=True)
    zc = z - mu
    var = jnp.mean(zc * zc, axis=-1, keepdims=True)
    return zc * lax.rsqrt(var + LN_EPS) * g + b


def _gelu(c):
    return 0.5 * c * (1.0 + lax.erf(c * (2.0 ** -0.5)))


def _dot(a, b):
    return jnp.dot(a, b, preferred_element_type=F32)


def _dot_nt(a, b):
    return lax.dot_general(a, b, (((1,), (1,)), ((), ())), preferred_element_type=F32)


def _dot_tn(a, b):
    return lax.dot_general(a, b, (((0,), (0,)), ((), ())), preferred_element_type=F32)


def _rope_tables(pos):
    inv = 1.0 / (ROPE_THETA ** (jnp.arange(0, HEAD_DIM, 2, dtype=F32) / HEAD_DIM))
    ang = pos[:, None] * inv[None, :]
    cos, sin = jnp.cos(ang), jnp.sin(ang)
    zero = jnp.zeros_like(sin)
    cos_h = jnp.concatenate([cos, cos], -1)
    sa_h = jnp.concatenate([-sin, zero], -1)
    sb_h = jnp.concatenate([zero, sin], -1)
    two = lambda t: jnp.concatenate([t, t], -1)
    return two(cos_h), two(sa_h), two(sb_h)


def _rope_slab(s, cos, sa, sb):
    return (s * cos + pltpu.roll(s, LANES - HEAD_DIM // 2, 1) * sa
            + pltpu.roll(s, HEAD_DIM // 2, 1) * sb)


HALF = HEAD_DIM // 2


def _paired_perm():
    base = jnp.arange(HALF)
    return jnp.concatenate([base, HEAD_DIM + base, HALF + base, HEAD_DIM + HALF + base])


def _rope_tables_paired(pos):
    inv = 1.0 / (ROPE_THETA ** (jnp.arange(0, HEAD_DIM, 2, dtype=F32) / HEAD_DIM))
    ang = pos[:, None] * inv[None, :]
    cos, sin = jnp.cos(ang), jnp.sin(ang)
    return (jnp.concatenate([cos] * 4, -1), jnp.concatenate([-sin, -sin, sin, sin], -1))


def _rope_paired(s, cos, sin_signed):
    return s * cos + pltpu.roll(s, HEAD_DIM, 1) * sin_signed


def _paired_split(t):
    lane = lax.broadcasted_iota(jnp.int32, t.shape, 1)
    on_a = (lane % HEAD_DIM) < HALF
    z = jnp.zeros_like(t)
    return (jnp.where(on_a, t, z), jnp.where(on_a, z, pltpu.roll(t, HALF, 1)),
            jnp.where(on_a, pltpu.roll(t, LANES - HALF, 1), z), jnp.where(on_a, z, t))


def _paired_to_standard(t):
    lane = lax.broadcasted_iota(jnp.int32, t.shape, 1)
    keep = (lane < HALF) | (lane >= LANES - HALF)
    return jnp.where(keep, t, jnp.where(lane < HEAD_DIM, pltpu.roll(t, LANES - HALF, 1),
                                        pltpu.roll(t, HALF, 1)))


def _sink_softmax(s, sink):
    m = jnp.maximum(jnp.max(s, axis=-1, keepdims=True), sink)
    p = jnp.exp(s - m)
    den = jnp.sum(p, axis=-1, keepdims=True) + jnp.exp(sink - m)
    return p * (1.0 / den)


def _lane_split_pair(t):
    lo = lax.broadcasted_iota(jnp.int32, t.shape, 1) < HEAD_DIM
    tr = pltpu.roll(t, HEAD_DIM, 1)
    z = jnp.zeros_like(t)
    return (jnp.where(lo, t, z), jnp.where(lo, z, tr), jnp.where(lo, tr, z), jnp.where(lo, z, t))


def _attn_prompt_kernel(sinks_ref, x_ref, cos_ref, sin_ref, wqkv_ref, bqkv_ref, wo_ref,
                        g_ref, b_ref, y_ref, kn_ref, vn_ref, kprev_ref, vprev_ref):
    ti = pl.program_id(1)
    nq = N_HEADS * HEAD_DIM
    w = WINDOW

    @pl.when(ti == 0)
    def _():
        kprev_ref[...] = jnp.zeros_like(kprev_ref)
        vprev_ref[...] = jnp.zeros_like(vprev_ref)

    nblk = x_ref.shape[1] // w
    nslab = nq // LANES
    spg = nslab // N_KV_HEADS

    t = lax.broadcasted_iota(jnp.int32, (w, w), 0)
    s = lax.broadcasted_iota(jnp.int32, (w, w), 1)
    upper = s > t
    has_prev = ti > 0

    def project(blk):
        rows = slice(blk * w, (blk + 1) * w)
        x = x_ref[0, rows, :]
        qkv = _dot(x.astype(BF16), wqkv_ref[...]) + bqkv_ref[...]
        cos, sin = cos_ref[rows, :], sin_ref[rows, :]
        k = _rope_paired(qkv[:, nq:nq + LANES], cos, sin)
        v = qkv[:, nq + LANES:nq + 2 * LANES]
        q = [(_rope_paired(qkv[:, j * LANES:(j + 1) * LANES], cos, sin) * ATTN_SCALE).astype(BF16)
             for j in range(nslab)]
        return dict(x=x, q=q, k=k, v=v, ks=_paired_split(k.astype(BF16)),
                    vs=_lane_split_pair(v.astype(BF16)))

    def finish(blk, st, pv):
        slabs = []
        for g in range(N_KV_HEADS):
            og = pv[g, 0] + pv[g, 1]
            slabs += [og[i * w:(i + 1) * w] for i in range(spg)]
        o = jnp.concatenate(slabs, axis=1).astype(BF16)
        z = ALPHA * st["x"] + _dot(o, wo_ref[...])
        y_ref[0, blk * w:(blk + 1) * w, :] = _layer_norm(z, g_ref[...], b_ref[...])

    units = [(g, par) for g in range(N_KV_HEADS) for par in range(2)]
    prev = dict(ks=_paired_split(kprev_ref[...].astype(BF16)),
                vs=_lane_split_pair(vprev_ref[...].astype(BF16)))
    stages = {-1: prev, 0: project(0)}
    pvs = {}
    for blk in range(nblk):
        if blk + 1 < nblk:
            stages[blk + 1] = project(blk + 1)
        st, pst = stages[blk], stages[blk - 1]

        def scores(u):
            g, par = u
            qg = jnp.concatenate([st["q"][g * spg + i] for i in range(spg)], axis=0)
            kmat = jnp.concatenate([pst["ks"][2 * g + par], st["ks"][2 * g + par]], axis=0)
            return _dot_nt(qg, kmat)

        pv = {}
        sc_nxt = scores(units[0])
        for n, (g, par) in enumerate(units):
            sc = sc_nxt
            if n + 1 < len(units):
                sc_nxt = scores(units[n + 1])
            ps = []
            for i in range(spg):
                head = 2 * (g * spg + i) + par
                s_prev = sc[i * w:(i + 1) * w, :w]
                if blk == 0:
                    s_prev = jnp.where(has_prev, s_prev, NEG_INF)
                p = _sink_softmax(jnp.where(upper, s_prev, sc[i * w:(i + 1) * w, w:]), sinks_ref[head])
                zero = jnp.zeros_like(p)
                ps.append(jnp.concatenate([jnp.where(upper, p, zero), jnp.where(upper, zero, p)],
                                          axis=1).astype(BF16))
            vmat = jnp.concatenate([pst["vs"][2 * g + par], st["vs"][2 * g + par]], axis=0)
            pv[g, par] = _dot(jnp.concatenate(ps, axis=0), vmat)
        pvs[blk] = pv
        if blk >= 1:
            finish(blk - 1, stages[blk - 1], pvs[blk - 1])
    finish(nblk - 1, stages[nblk - 1], pvs[nblk - 1])

    k_last, v_last = stages[nblk - 1]["k"], stages[nblk - 1]["v"]
    kprev_ref[...] = k_last
    vprev_ref[...] = v_last

    @pl.when(ti == pl.num_programs(1) - 1)
    def _():
        kn_ref[0] = _paired_to_standard(k_last)
        vn_ref[0] = v_last


def _attn_prompt(x, wqkv, bqkv, sinks, wo, g, b):
    bsz, seq, d = x.shape
    tq = min(ATTN_TQ, seq)
    nqkv = wqkv.shape[1]
    nq = N_HEADS * HEAD_DIM
    perm = _paired_perm()
    cols = jnp.concatenate([s * LANES + perm for s in range((nq + LANES) // LANES)]
                           + [jnp.arange(nq + LANES, nqkv)])
    wqkv, bqkv = wqkv[:, cols], bqkv[:, cols]
    cos, sin = _rope_tables_paired(jnp.arange(seq, dtype=F32))
    tab = pl.BlockSpec((tq, LANES), lambda bi, ti: (ti, 0))
    kv_out = pl.BlockSpec((1, WINDOW, LANES), lambda bi, ti: (bi, 0, 0))
    return pl.pallas_call(
        _attn_prompt_kernel,
        grid=(bsz, seq // tq),
        in_specs=[
            pl.BlockSpec(memory_space=pltpu.SMEM),
            pl.BlockSpec((1, tq, d), lambda bi, ti: (bi, ti, 0)),
            tab, tab,
            _resident((d, nqkv)), _resident((1, nqkv)), _resident((d, d)),
            _resident((1, d)), _resident((1, d)),
        ],
        out_specs=[pl.BlockSpec((1, tq, d), lambda bi, ti: (bi, ti, 0)), kv_out, kv_out],
        out_shape=[jax.ShapeDtypeStruct((bsz, seq, d), F32),
                   jax.ShapeDtypeStruct((bsz, WINDOW, LANES), F32),
                   jax.ShapeDtypeStruct((bsz, WINDOW, LANES), F32)],
        scratch_shapes=[pltpu.VMEM((WINDOW, LANES), F32), pltpu.VMEM((WINDOW, LANES), F32)],
        compiler_params=_cparams(("arbitrary", "arbitrary")),
        name="attn_prompt",
    )(sinks, x, cos, sin, wqkv, bqkv, wo, g, b)


def _conv_gate(a, a1, a2, u, cw, cb):
    c = cb + cw[0:1] * a2 + cw[1:2] * a1 + cw[2:3] * a
    return (_gelu(c) * u).astype(BF16)


def _ffn_prompt_kernel(x_ref, wa_ref, wu_ref, wd_ref, cw_ref, cb_ref, g_ref, b_ref,
                       y_ref, cs_ref, carry_ref, h_ref, *, tiles_per_seq):
    i = pl.program_id(0)
    pos = i % tiles_per_seq

    @pl.when(pos == 0)
    def _():
        carry_ref[...] = jnp.zeros_like(carry_ref)

    x = x_ref[...]
    xb = x.astype(BF16)
    tm = x.shape[0]
    dff = wd_ref.shape[0]
    tf = FFN_TF
    row = lax.broadcasted_iota(jnp.int32, (SUBLANES, tf), 0)
    for j in range(dff // tf):
        sl = slice(j * tf, (j + 1) * tf)
        a = _dot(xb, wa_ref[:, sl])
        u = _dot(xb, wu_ref[:, sl])
        prev = carry_ref[:, sl]
        p1 = prev[SUBLANES - 1:SUBLANES]
        p2 = prev[SUBLANES - 2:SUBLANES - 1]
        a1 = pltpu.roll(a, 1, 0)
        a2 = pltpu.roll(a, 2, 0)
        a1_top = jnp.where(row == 0, p1, a1[:SUBLANES])
        a2_top = jnp.where(row == 0, p2, jnp.where(row == 1, p1, a2[:SUBLANES]))
        a1 = jnp.concatenate([a1_top, a1[SUBLANES:]], axis=0)
        a2 = jnp.concatenate([a2_top, a2[SUBLANES:]], axis=0)
        carry_ref[:, sl] = a[tm - SUBLANES:tm]
        h_ref[:, sl] = _conv_gate(a, a1, a2, u, cw_ref[:, sl], cb_ref[:, sl])

    rt = tm // FFN_ROW_SPLITS
    for r in range(FFN_ROW_SPLITS):
        rows = slice(r * rt, (r + 1) * rt)
        z = ALPHA * x[rows] + _dot(h_ref[rows, :], wd_ref[...])
        y_ref[rows, :] = _layer_norm(z, g_ref[...], b_ref[...])

    @pl.when(pos == tiles_per_seq - 1)
    def _():
        cs_ref[0] = carry_ref[SUBLANES - (CONV_W - 1):SUBLANES, :]


def _ffn_prompt(x, seq, layer, wa, wu, wd, cw, cb, g, b):
    n, d = x.shape
    tm = min(FFN_TM, seq)
    tiles_per_seq = seq // tm
    dff = wd.shape[0]
    return pl.pallas_call(
        functools.partial(_ffn_prompt_kernel, tiles_per_seq=tiles_per_seq),
        grid=(n // tm,),
        in_specs=[pl.BlockSpec((tm, d), lambda i: (i, 0))]
        + [_resident(a.shape) for a in (wa, wu, wd)]
        + [_resident_layer(a, layer) for a in (cw, cb, g, b)],
        out_specs=[pl.BlockSpec((tm, d), lambda i: (i, 0)),
                   pl.BlockSpec((1, CONV_W - 1, dff), lambda i: (i // tiles_per_seq, 0, 0))],
        out_shape=[jax.ShapeDtypeStruct((n, d), F32),
                   jax.ShapeDtypeStruct((n // seq, CONV_W - 1, dff), F32)],
        scratch_shapes=[pltpu.VMEM((SUBLANES, dff), F32), pltpu.VMEM((tm, dff), BF16)],
        compiler_params=_cparams(("arbitrary",)),
        name="ffn_prompt",
    )(x, wa, wu, wd, cw, cb, g, b)


def _ffn_sample_kernel(x_ref, s0_ref, s1_ref, wa_ref, wu_ref, wd_ref, cw_ref, cb_ref, g_ref, b_ref,
                       y_ref, a_ref, wab_ref, wub_ref, wdb_ref, acc_ref):
    j = pl.program_id(0)
    x = x_ref[...]
    xb = x.astype(BF16)
    wa, wu, wd = wa_ref[...].astype(BF16), wu_ref[...].astype(BF16), wd_ref[...].astype(BF16)
    wab_ref[...] = wa
    wub_ref[...] = wu
    wdb_ref[...] = wd
    a = _dot(xb, wa)
    u = _dot(xb, wu)
    a_ref[...] = a
    h = _conv_gate(a, s1_ref[...], s0_ref[...], u, cw_ref[...], cb_ref[...])
    dn = _dot(h, wd)

    @pl.when(j == 0)
    def _():
        acc_ref[...] = dn

    @pl.when(j > 0)
    def _():
        acc_ref[...] += dn

    @pl.when(j == pl.num_programs(0) - 1)
    def _():
        y_ref[...] = _layer_norm(ALPHA * x + acc_ref[...], g_ref[...], b_ref[...])


def _ffn_sample(x, s0, s1, layer, wup, wd, cw, cb, g, b):
    n, d = x.shape
    dff = wd.shape[1]
    tf = FFN_TF
    nchunk = dff // tf
    col = pl.BlockSpec((n, tf), lambda j: (0, j))
    return pl.pallas_call(
        _ffn_sample_kernel,
        grid=(nchunk,),
        in_specs=[
            pl.BlockSpec((n, d), lambda j: (0, 0)), col, col,
            pl.BlockSpec((None, d, tf), lambda j: (layer, 0, j)),
            pl.BlockSpec((None, d, tf), lambda j: (layer, 0, j + nchunk)),
            pl.BlockSpec((None, tf, d), lambda j: (layer, j, 0)),
            pl.BlockSpec((None, CONV_W, tf), lambda j: (layer, 0, j)),
            pl.BlockSpec((None, 1, tf), lambda j: (layer, 0, j)),
            pl.BlockSpec((None, 1, d), lambda j: (layer, 0, 0)),
            pl.BlockSpec((None, 1, d), lambda j: (layer, 0, 0)),
        ],
        out_specs=[pl.BlockSpec((n, d), lambda j: (0, 0)), col,
                   pl.BlockSpec((d, tf), lambda j: (0, j)), pl.BlockSpec((d, tf), lambda j: (0, j)),
                   pl.BlockSpec((tf, d), lambda j: (j, 0))],
        out_shape=[jax.ShapeDtypeStruct((n, d), F32), jax.ShapeDtypeStruct((n, dff), F32),
                   jax.ShapeDtypeStruct((d, dff), BF16), jax.ShapeDtypeStruct((d, dff), BF16),
                   jax.ShapeDtypeStruct((dff, d), BF16)],
        scratch_shapes=[pltpu.VMEM((n, d), F32)],
        compiler_params=_cparams(("arbitrary",)),
        name="ffn_sample",
    )(x, s0, s1, wup, wup, wd, cw, cb, g, b)


def _gla_decay_logits(xb, wa1_ref, wa2_ref, ba_ref):
    t1 = _dot(xb, wa1_ref[...])
    z = _dot(t1.astype(BF16), wa2_ref[...]) + ba_ref[...]
    return jax.nn.log_sigmoid(z) * (1.0 / GLA_TAU)


def _gla_out_gate(o, gate, ng):
    outs = []
    for h in range(GLA_HEADS):
        oh = o[:, h * GLA_DVH:(h + 1) * GLA_DVH]
        ms = jnp.mean(oh * oh, axis=-1, keepdims=True)
        outs.append(oh * lax.rsqrt(ms + NORM_EPS) * ng)
    on = jnp.concatenate(outs, axis=1)
    return (on * jax.nn.silu(gate)).astype(BF16)


def _split3(t):
    hi = t.astype(BF16)
    r1 = t - hi.astype(F32)
    mid = r1.astype(BF16)
    lo = (r1 - mid.astype(F32)).astype(BF16)
    return hi, mid, lo


def _gla_prompt_kernel(x_ref, win_ref, wa1_ref, wa2_ref, ba_ref, ng_ref, wo_ref, g_ref, b_ref,
                       y_ref, so_ref, s_ref, proj_ref, u_ref, o_ref):
    ti = pl.program_id(1)
    ck = GLA_CHUNK

    @pl.when(ti == 0)
    def _():
        s_ref[...] = jnp.zeros_like(s_ref)

    x = x_ref[0]
    xb = x.astype(BF16)
    tm = x.shape[0]
    nck = tm // ck
    la = _gla_decay_logits(xb, wa1_ref, wa2_ref, ba_ref)
    c_v, c_g = 2 * GLA_DK, 2 * GLA_DK + GLA_DV
    proj_ref[:, :c_v] = _dot(xb, win_ref[:, :c_v])
    la_parts = _split3(la)

    r = lax.broadcasted_iota(jnp.int32, (ck, ck), 0)
    c = lax.broadcasted_iota(jnp.int32, (ck, ck), 1)
    causal = r >= c
    tril = causal.astype(BF16)

    tril3 = jnp.concatenate([tril] * 3 + [jnp.zeros_like(tril)], axis=1)
    zrows = jnp.zeros((ck, GLA_DK), BF16)
    bcs = [_dot(tril3, jnp.concatenate([part[ci * ck:(ci + 1) * ck] for part in la_parts] + [zrows],
                                       axis=0))
           for ci in range(nck)]

    proj_ref[:, c_v:c_g] = _dot(xb, win_ref[:, c_v:c_g])
    q_in, vb, att, dec = {}, {}, {}, {}
    for ci in range(nck):
        rows = slice(ci * ck, (ci + 1) * ck)
        bc = bcs[ci]
        for h in range(GLA_HEADS):
            bh = bc[:, h * GLA_DKH:(h + 1) * GLA_DKH]
            q = proj_ref[rows, h * GLA_DKH:(h + 1) * GLA_DKH] * (GLA_DKH ** -0.5)
            k = proj_ref[rows, GLA_DK + h * GLA_DKH:GLA_DK + (h + 1) * GLA_DKH]
            v = proj_ref[rows, 2 * GLA_DK + h * GLA_DVH:2 * GLA_DK + (h + 1) * GLA_DVH].astype(BF16)
            qi = (q * jnp.exp(bh)).astype(BF16)
            k_in = (k * jnp.exp(-bh)).astype(BF16)
            bl = bh[ck - 1:ck]
            kd = (k * jnp.exp(bl - bh)).astype(BF16)
            att[ci, h] = _dot_nt(qi, k_in)
            u_ref[ci, h] = _dot_tn(kd, v)
            q_in[ci, h], vb[ci, h], dec[ci, h] = qi, v, jnp.exp(bl)

    proj_ref[:, c_g:] = _dot(xb, win_ref[:, c_g:])
    state = [s_ref[h] for h in range(GLA_HEADS)]
    zpad_l = jnp.zeros((ck, ck), BF16)
    zpad_r = jnp.zeros((ck, GLA_DVH), BF16)
    for ci in range(nck):
        rows = slice(ci * ck, (ci + 1) * ck)
        for h in range(GLA_HEADS):
            am = jnp.where(causal, att[ci, h], 0.0).astype(BF16)
            lhs = jnp.concatenate([q_in[ci, h], am, zpad_l], axis=1)
            rhs = jnp.concatenate([state[h].astype(BF16), vb[ci, h], zpad_r], axis=0)
            o_ref[rows, h * GLA_DVH:(h + 1) * GLA_DVH] = _dot(lhs, rhs)
            dcol = jnp.transpose(jnp.broadcast_to(dec[ci, h], (GLA_DKH, GLA_DKH)))
            dfull = jnp.concatenate([dcol] * (GLA_DVH // GLA_DKH), axis=1)
            state[h] = dfull * state[h] + u_ref[ci, h]
    for h in range(GLA_HEADS):
        s_ref[h] = state[h]

    rt = tm // GLA_ROW_SPLITS
    for rs in range(GLA_ROW_SPLITS):
        rows = slice(rs * rt, (rs + 1) * rt)
        og = _gla_out_gate(o_ref[rows, :], proj_ref[rows, 2 * GLA_DK + GLA_DV:], ng_ref[...])
        z = ALPHA * x[rows] + _dot(og, wo_ref[...])
        y_ref[0, rows, :] = _layer_norm(z, g_ref[...], b_ref[...])

    @pl.when(ti == pl.num_programs(1) - 1)
    def _():
        so_ref[0] = s_ref[...]


def _gla_prompt(x, win, wa1, wa2, ba, ng, wo, g, b):
    bsz, seq, d = x.shape
    tm = min(GLA_TM, seq)
    nin = win.shape[1]
    st = (GLA_HEADS, GLA_DKH, GLA_DVH)
    return pl.pallas_call(
        _gla_prompt_kernel,
        grid=(bsz, seq // tm),
        in_specs=[
            pl.BlockSpec((1, tm, d), lambda bi, ti: (bi, ti, 0)),
            _resident(win.shape), _resident(wa1.shape), _resident(wa2.shape), _resident(ba.shape),
            _resident(ng.shape), _resident(wo.shape), _resident(g.shape), _resident(b.shape),
        ],
        out_specs=[pl.BlockSpec((1, tm, d), lambda bi, ti: (bi, ti, 0)),
                   pl.BlockSpec((1,) + st, lambda bi, ti: (bi, 0, 0, 0))],
        out_shape=[jax.ShapeDtypeStruct((bsz, seq, d), F32),
                   jax.ShapeDtypeStruct((bsz,) + st, F32)],
        scratch_shapes=[pltpu.VMEM(st, F32), pltpu.VMEM((tm, nin), F32),
                        pltpu.VMEM((tm // GLA_CHUNK,) + st, F32), pltpu.VMEM((tm, GLA_DV), F32)],
        compiler_params=_cparams(("arbitrary", "arbitrary")),
        name="gla_prompt",
    )(x, win, wa1, wa2, ba, ng, wo, g, b)


def _linear_kernel(x_ref, w_ref, b_ref, o_ref):
    o_ref[...] = _dot(x_ref[...].astype(BF16), w_ref[...]) + b_ref[...]


def _linear_cast_kernel(x_ref, w_ref, b_ref, o_ref, wb_ref):
    wb = w_ref[...].astype(BF16)
    wb_ref[...] = wb
    o_ref[...] = _dot(x_ref[...].astype(BF16), wb) + b_ref[...]


def _linear(x, w, bias):
    n, kdim = x.shape
    nout = w.shape[1]
    tn = next(t for t in (512, 256, 128) if nout % t == 0)
    emit = w.dtype != BF16
    wspec = pl.BlockSpec((kdim, tn), lambda j: (0, j))
    out_specs = [pl.BlockSpec((n, tn), lambda j: (0, j))]
    out_shape = [jax.ShapeDtypeStruct((n, nout), F32)]
    if emit:
        out_specs.append(wspec)
        out_shape.append(jax.ShapeDtypeStruct(w.shape, BF16))
    outs = pl.pallas_call(
        _linear_cast_kernel if emit else _linear_kernel,
        grid=(nout // tn,),
        in_specs=[pl.BlockSpec((n, kdim), lambda j: (0, 0)), wspec,
                  pl.BlockSpec((1, tn), lambda j: (0, j))],
        out_specs=out_specs,
        out_shape=out_shape,
        compiler_params=_cparams(("arbitrary",)),
        name="sample_linear",
    )(x, w, bias)
    return tuple(outs) if emit else outs[0]


def _linear_ln_kernel(h_ref, w_ref, r_ref, g_ref, b_ref, o_ref, wb_ref):
    wb = w_ref[...].astype(BF16)
    wb_ref[...] = wb
    z = ALPHA * r_ref[...] + _dot(h_ref[...].astype(BF16), wb)
    o_ref[...] = _layer_norm(z, g_ref[...], b_ref[...])


def _linear_ln(h, w, resid, g, b):
    n, d = resid.shape
    full = lambda a: pl.BlockSpec(a.shape, lambda i: (0,) * a.ndim)
    return pl.pallas_call(
        _linear_ln_kernel,
        grid=(1,),
        in_specs=[full(h), full(w), full(resid), full(g), full(b)],
        out_specs=[pl.BlockSpec((n, d), lambda i: (0, 0)), full(w)],
        out_shape=[jax.ShapeDtypeStruct((n, d), F32), jax.ShapeDtypeStruct(w.shape, BF16)],
        compiler_params=_cparams(("arbitrary",)),
        name="sample_linear_ln",
    )(h, w, resid, g, b)


def _attn_sample_kernel(q2_ref, kv_ref, ck_ref, cv_ref, cos_ref, sa_ref, sb_ref, sink_ref,
                        o2_ref, kn_ref, vn_ref):
    nseq, wbuf, _ = ck_ref.shape
    cos, sa, sb = cos_ref[...], sa_ref[...], sb_ref[...]
    q2 = (_rope_slab(q2_ref[...], cos, sa, sb) * ATTN_SCALE).astype(BF16)
    kv = kv_ref[...]
    k_all = _rope_slab(kv[:, :LANES], cos, sa, sb)
    v_all = kv[:, LANES:]
    last = lax.broadcasted_iota(jnp.int32, (wbuf, LANES), 0) == wbuf - 1
    sink = sink_ref[...]

    k_new, v_new = [], []
    for s in range(nseq):
        k_new.append(jnp.where(last, k_all[s:s + 1], pltpu.roll(ck_ref[s], wbuf - 1, 0)))
        v_new.append(jnp.where(last, v_all[s:s + 1], pltpu.roll(cv_ref[s], wbuf - 1, 0)))
        kn_ref[s] = k_new[s]
        vn_ref[s] = v_new[s]
    sc = [_dot_nt(q2[s * N_HEADS:(s + 1) * N_HEADS], k_new[s].astype(BF16)) for s in range(nseq)]
    ps = [_sink_softmax(sc[s], sink).astype(BF16) for s in range(nseq)]
    for s in range(nseq):
        o2_ref[s * N_HEADS:(s + 1) * N_HEADS, :] = _dot(ps[s], v_new[s].astype(BF16))


def _attn_sample(qkv, ck, cv, sinks):
    n = qkv.shape[0]
    wbuf = ck.shape[1]
    sb_ = SAMPLE_SEQ_BLOCK
    nq = N_HEADS * HEAD_DIM
    cos, sa, sb = _rope_tables(jnp.full((1,), PAST_LEN, dtype=F32))
    one = lambda a: pl.BlockSpec(a.shape, lambda i: (0,) * a.ndim)
    sink_col = sinks.reshape(N_HEADS, 1)
    q = qkv[:, :nq].reshape(n, N_HEADS, HEAD_DIM)
    zero = jnp.zeros_like(q)
    first_group = (jnp.arange(N_HEADS) < GROUP)[None, :, None]
    q2 = jnp.concatenate([jnp.where(first_group, q, zero), jnp.where(first_group, zero, q)], axis=-1)
    q2 = q2.reshape(n * N_HEADS, LANES)
    cache = pl.BlockSpec((sb_, wbuf, LANES), lambda i: (i, 0, 0))
    rows16 = pl.BlockSpec((sb_ * N_HEADS, LANES), lambda i: (i, 0))
    o2, k_s, v_s = pl.pallas_call(
        _attn_sample_kernel,
        grid=(n // sb_,),
        in_specs=[rows16, pl.BlockSpec((sb_, 2 * LANES), lambda i: (i, 0)), cache, cache,
                  one(cos), one(sa), one(sb), one(sink_col)],
        out_specs=[rows16, cache, cache],
        out_shape=[jax.ShapeDtypeStruct((n * N_HEADS, LANES), F32),
                   jax.ShapeDtypeStruct(ck.shape, F32), jax.ShapeDtypeStruct(cv.shape, F32)],
        compiler_params=_cparams(("arbitrary",)),
        name="attn_sample",
    )(q2, qkv[:, nq:], ck, cv, cos, sa, sb, sink_col)
    o2 = o2.reshape(n, N_HEADS, N_KV_HEADS, HEAD_DIM)
    o = jnp.where(first_group, o2[:, :, 0, :], o2[:, :, 1, :])
    return o.reshape(n, nq), k_s, v_s


def _gla_sample_kernel(proj_ref, z_ref, s0_ref, ng_ref, o_ref, sn_ref, oacc_ref):
    proj = proj_ref[...]
    la = jax.nn.log_sigmoid(z_ref[...]) * (1.0 / GLA_TAU)
    nseq = proj.shape[0]
    pad = jnp.zeros((LANES - nseq, GLA_DKH), F32)

    def cols(t):
        return jnp.transpose(jnp.concatenate([t, pad], axis=0))

    for h in range(GLA_HEADS):
        b = la[:, h * GLA_DKH:(h + 1) * GLA_DKH]
        q = proj[:, h * GLA_DKH:(h + 1) * GLA_DKH] * (GLA_DKH ** -0.5)
        k = proj[:, GLA_DK + h * GLA_DKH:GLA_DK + (h + 1) * GLA_DKH]
        v = proj[:, 2 * GLA_DK + h * GLA_DVH:2 * GLA_DK + (h + 1) * GLA_DVH]
        eb = jnp.exp(b)
        q_in = q * eb
        k_in = k * jnp.exp(-b)
        att = jnp.sum(q_in * k_in, axis=-1, keepdims=True)
        kd = k * jnp.exp(b - b)
        qc, kc, ec = cols(q_in), cols(kd), cols(eb)
        for s in range(nseq):
            s_old = s0_ref[s, h]
            vrow = v[s:s + 1]
            o_row = att[s:s + 1] * vrow + jnp.sum(qc[:, s:s + 1] * s_old, axis=0, keepdims=True)
            oacc_ref[s:s + 1, h * GLA_DVH:(h + 1) * GLA_DVH] = o_row
            sn_ref[s, h] = ec[:, s:s + 1] * s_old + kc[:, s:s + 1] * vrow
    og = _gla_out_gate(oacc_ref[...], proj[:, 2 * GLA_DK + GLA_DV:], ng_ref[...])
    o_ref[...] = og.astype(F32)


def _gla_sample(proj, z, s0, ng):
    n = proj.shape[0]
    sb_ = SAMPLE_SEQ_BLOCK
    st = pl.BlockSpec((sb_,) + s0.shape[1:], lambda i: (i, 0, 0, 0))
    return pl.pallas_call(
        _gla_sample_kernel,
        grid=(n // sb_,),
        in_specs=[pl.BlockSpec((sb_, proj.shape[1]), lambda i: (i, 0)),
                  pl.BlockSpec((sb_, z.shape[1]), lambda i: (i, 0)), st,
                  pl.BlockSpec(ng.shape, lambda i: (0, 0))],
        out_specs=[pl.BlockSpec((sb_, GLA_DV), lambda i: (i, 0)), st],
        out_shape=[jax.ShapeDtypeStruct((n, GLA_DV), F32), jax.ShapeDtypeStruct(s0.shape, F32)],
        scratch_shapes=[pltpu.VMEM((sb_, GLA_DV), F32)],
        compiler_params=_cparams(("arbitrary",)),
        name="gla_sample",
    )(proj, z, s0, ng)


def _row(v):
    return v.reshape(1, -1)


def kernel(x_prompt, x_sample, cache_k, cache_v, state_gla, state_conv, attn_w_qkv, attn_b_qkv, attn_sinks, attn_w_o, gla_w_in, gla_w_a1, gla_w_a2, gla_b_a, gla_norm_g, gla_w_o, ffn_w_up, ffn_conv_w, ffn_conv_b, ffn_w_down, ln_mix_g, ln_mix_b, ln_ffn_g, ln_ffn_b):
    bsz, seq, d = x_prompt.shape
    nsmp = x_sample.shape[0]
    xp = x_prompt
    xs = x_sample.reshape(nsmp, d)
    zeros = lambda n: jnp.zeros((1, n), F32)
    ffn_small = (ffn_conv_w, ffn_conv_b[:, None, :], ln_ffn_g[:, None, :], ln_ffn_b[:, None, :])

    kp_l, vp_l, ks_l, vs_l, sp_l, ss_l, cp_l, cs_l = [], [], [], [], [], [], [], []
    for i in range(DEPTH):
        j = i // 2
        g_mix, b_mix = _row(ln_mix_g[i]), _row(ln_mix_b[i])
        if i % 2 == 0:
            bqkv = _row(attn_b_qkv[j])
            wbuf = cache_k.shape[2]
            qkv_s, wqkv = _linear(xs, attn_w_qkv[j], bqkv)
            o_s, k_s, v_s = _attn_sample(qkv_s, cache_k[j].reshape(nsmp, wbuf, LANES),
                                         cache_v[j].reshape(nsmp, wbuf, LANES), attn_sinks[j])
            xs, wo = _linear_ln(o_s, attn_w_o[j], xs, g_mix, b_mix)
            ks_l.append(k_s.reshape(cache_k.shape[1:]))
            vs_l.append(v_s.reshape(cache_v.shape[1:]))

            xp, kp, vp = _attn_prompt(xp, wqkv, bqkv, attn_sinks[j], wo, g_mix, b_mix)
            kv_shape = (bsz, WINDOW, N_KV_HEADS, HEAD_DIM)
            kp_l.append(kp.reshape(kv_shape))
            vp_l.append(vp.reshape(kv_shape))
        else:
            wa1 = jnp.pad(gla_w_a1[j], ((0, 0), (0, LANES - GLA_RANK))).astype(BF16)
            wa2 = jnp.pad(gla_w_a2[j], ((0, LANES - GLA_RANK), (0, 0))).astype(BF16)
            ba = _row(gla_b_a[j])
            ng = _row(gla_norm_g[j])
            proj_s, win = _linear(xs, gla_w_in[j], zeros(gla_w_in.shape[2]))
            t1 = _linear(xs, wa1, zeros(LANES))
            z_s = _linear(t1, wa2, ba)
            o_s, s_s = _gla_sample(proj_s, z_s, state_gla[j], ng)
            xs, wo = _linear_ln(o_s, gla_w_o[j], xs, g_mix, b_mix)
            ss_l.append(s_s)

            xp, sp = _gla_prompt(xp, win, wa1, wa2, ba, ng, wo, g_mix, b_mix)
            sp_l.append(sp)

        xs, a_s, wa, wu, wd = _ffn_sample(xs, state_conv[i, :, 0], state_conv[i, :, 1], i,
                                          ffn_w_up, ffn_w_down, *ffn_small)
        cs_l.append(jnp.stack([state_conv[i, :, 1], a_s], axis=1))
        xp2, cp = _ffn_prompt(xp.reshape(bsz * seq, d), seq, i, wa, wu, wd, *ffn_small)
        xp = xp2.reshape(bsz, seq, d)
        cp_l.append(cp)

    return (xp, xs.reshape(x_sample.shape), jnp.stack(kp_l), jnp.stack(vp_l), jnp.stack(ks_l),
            jnp.stack(vs_l), jnp.stack(sp_l), jnp.stack(ss_l), jnp.stack(cp_l), jnp.stack(cs_l))
```
